```python
import jax, jax.numpy as jnp
from jax import lax
import numpy as np

D_MODEL = 1024
BATCH = 4
SEQ = 8192
DEPTH = 2

GRID_W = 64
CTX_LEN = 256
EPS = 1e-6
D_FF = 2816
LRU_WIDTH = 512
LRU_HEADS = 8
LRU_HEAD_DIM = LRU_WIDTH // LRU_HEADS
CONV_WIDTH = 4
CONV_LEFT = 2
RG_C = 8.0
MLP_GROUPS = 4
MLP_GROUP_DIM = 128
MLP_WIDTH = MLP_GROUPS * MLP_GROUP_DIM
CHUNK = 128
ROWS_PER_CHUNK = CHUNK // GRID_W
MIX_WIDTH = LRU_WIDTH + MLP_WIDTH
IN_PROJ_WIDTH = 2 * LRU_WIDTH + 2 * MLP_WIDTH
N_MOD = 9

kernel_name = "hybrid_rglru_chunkmlp_macaron_dit"


def rmsnorm(x, g):
    x32 = x.astype(jnp.float32)
    y = x32 * lax.rsqrt(jnp.mean(x32 * x32, axis=-1, keepdims=True) + EPS)
    return (y * g.astype(jnp.float32)).astype(x.dtype)


def layernorm(x, g):
    x32 = x.astype(jnp.float32)
    mu = jnp.mean(x32, axis=-1, keepdims=True)
    xc = x32 - mu
    y = xc * lax.rsqrt(jnp.mean(xc * xc, axis=-1, keepdims=True) + EPS)
    return (y * g.astype(jnp.float32)).astype(x.dtype)


def modulate(z, shift, scale):
    return z * (1.0 + scale) + shift


def swiglu(z, w_in, w_out):
    gate, up = jnp.split(z @ w_in, 2, axis=-1)
    return (jax.nn.silu(gate) * up) @ w_out


def short_conv(x, w, b):
    n = x.shape[1]
    xp = jnp.pad(x, ((0, 0), (CONV_LEFT, CONV_WIDTH - 1 - CONV_LEFT), (0, 0)))
    y = b
    for k in range(CONV_WIDTH):
        y = y + xp[:, k:k + n] * w[k]
    return y


def rglru_coeffs(xc, w_r, b_r, w_i, b_i, lam):
    bsz, n, _ = xc.shape
    xh = xc.reshape(bsz, n, LRU_HEADS, LRU_HEAD_DIM)
    r = jax.nn.sigmoid(jnp.einsum("bnhd,hde->bnhe", xh, w_r).reshape(bsz, n, LRU_WIDTH) + b_r)
    i = jax.nn.sigmoid(jnp.einsum("bnhd,hde->bnhe", xh, w_i).reshape(bsz, n, LRU_WIDTH) + b_i)
    log_a = (-RG_C * r * jax.nn.softplus(-lam)).astype(jnp.float32)
    a = jnp.exp(log_a)
    b = jnp.sqrt(-jnp.expm1(2.0 * log_a)) * (i * xc).astype(jnp.float32)
    return a, b


def _combine(left, right):
    a_l, b_l = left
    a_r, b_r = right
    return a_l * a_r, a_r * b_l + b_r


def linear_scan(a, b, h0, reverse):
    a_cum, h = lax.associative_scan(_combine, (a, b), reverse=reverse, axis=1)
    return h + a_cum * h0[:, None, :]


def lru_scans(xl, conv_w, conv_b, w_r, b_r, w_i, b_i, lam, h0_f, h0_b):
    xc = short_conv(xl, conv_w, conv_b)
    a_f, b_f = rglru_coeffs(xc, w_r[0], b_r[0], w_i[0], b_i[0], lam[0])
    h_f = linear_scan(a_f, b_f, h0_f, reverse=False)
    a_b, b_b = rglru_coeffs(xc, w_r[1], b_r[1], w_i[1], b_i[1], lam[1])
    h_b = linear_scan(a_b, b_b, h0_b, reverse=True)
    return h_f, h_b


def spatial_gating(u, v, norm_g, w_s, b_s, n_chunks):
    bsz = v.shape[0]
    v = layernorm(v, norm_g)
    vc = v.reshape(bsz, n_chunks, CHUNK, MLP_GROUPS, MLP_GROUP_DIM)
    z = jnp.einsum("gpq,bnqgc->bnpgc", w_s, vc) + b_s.T[None, None, :, :, None]
    return u * z.reshape(u.shape)


def mix_out(p, h_f, h_b, sgu_norm_g, sgu_w, sgu_b, w_out, n_chunks):
    gl = p[..., LRU_WIDTH:2 * LRU_WIDTH]
    u = p[..., 2 * LRU_WIDTH:2 * LRU_WIDTH + MLP_WIDTH]
    v = p[..., 2 * LRU_WIDTH + MLP_WIDTH:]
    y_lru = (h_f + h_b).astype(p.dtype) * jax.nn.gelu(gl)
    y_sgu = spatial_gating(u, v, sgu_norm_g, sgu_w, sgu_b, n_chunks)
    return jnp.concatenate([y_lru, y_sgu], axis=-1) @ w_out


def setup_inputs(seed: int = 0) -> dict:
    key = jax.random.key(seed)
    ks = jax.random.split(key, 32)
    f32 = jnp.float32

    def nrm(k, shape, s):
        return jax.random.normal(k, shape, f32) * s

    def gain(k, shape):
        return 1.0 + 0.02 * jax.random.normal(k, shape, f32)

    u = jax.random.uniform(ks[20], (DEPTH, 2, LRU_WIDTH), f32, minval=0.9, maxval=0.999)
    a0 = u ** (1.0 / RG_C)
    lru_lambda = jnp.log(a0) - jnp.log1p(-a0)

    return {
        "x": nrm(ks[0], (BATCH, SEQ, D_MODEL), 1.0),
        "c": nrm(ks[1], (BATCH, D_MODEL), 1.0),
        "ctx": nrm(ks[2], (BATCH, CTX_LEN, D_MODEL), 1.0),
        "c_ctx": nrm(ks[3], (D_MODEL,), 1.0),
        "w_ada": nrm(ks[4], (DEPTH, D_MODEL, N_MOD * D_MODEL), 0.5 * D_MODEL ** -0.5),
        "b_ada": nrm(ks[5], (DEPTH, N_MOD * D_MODEL), 0.02),
        "ffn1_norm_g": gain(ks[6], (DEPTH, D_MODEL)),
        "ffn1_w_in": nrm(ks[7], (DEPTH, D_MODEL, 2 * D_FF), D_MODEL ** -0.5),
        "ffn1_w_out": nrm(ks[8], (DEPTH, D_FF, D_MODEL), D_FF ** -0.5),
        "mix_norm_g": gain(ks[9], (DEPTH, D_MODEL)),
        "w_in_mix": nrm(ks[10], (DEPTH, D_MODEL, IN_PROJ_WIDTH), D_MODEL ** -0.5),
        "lru_conv_w": nrm(ks[11], (DEPTH, CONV_WIDTH, LRU_WIDTH), CONV_WIDTH ** -0.5),
        "lru_conv_b": nrm(ks[12], (DEPTH, LRU_WIDTH), 0.02),
        "lru_w_r": nrm(ks[13], (DEPTH, 2, LRU_HEADS, LRU_HEAD_DIM, LRU_HEAD_DIM), LRU_HEAD_DIM ** -0.5),
        "lru_b_r": nrm(ks[14], (DEPTH, 2, LRU_WIDTH), 0.02),
        "lru_w_i": nrm(ks[15], (DEPTH, 2, LRU_HEADS, LRU_HEAD_DIM, LRU_HEAD_DIM), LRU_HEAD_DIM ** -0.5),
        "lru_b_i": nrm(ks[16], (DEPTH, 2, LRU_WIDTH), 0.02),
        "lru_lambda": lru_lambda,
        "sgu_norm_g": gain(ks[17], (DEPTH, MLP_WIDTH)),
        "sgu_w": nrm(ks[18], (DEPTH, MLP_GROUPS, CHUNK, CHUNK), CHUNK ** -0.5),
        "sgu_b": nrm(ks[19], (DEPTH, MLP_GROUPS, CHUNK), 0.02),
        "w_out_mix": nrm(ks[21], (DEPTH, MIX_WIDTH, D_MODEL), MIX_WIDTH ** -0.5),
        "ffn2_norm_g": gain(ks[22], (DEPTH, D_MODEL)),
        "ffn2_w_in": nrm(ks[23], (DEPTH, D_MODEL, 2 * D_FF), D_MODEL ** -0.5),
        "ffn2_w_out": nrm(ks[24], (DEPTH, D_FF, D_MODEL), D_FF ** -0.5),
        "final_norm_g": gain(ks[25], (D_MODEL,)),
    }


def reference(x, c, ctx, c_ctx, w_ada, b_ada, ffn1_norm_g, ffn1_w_in, ffn1_w_out,
              mix_norm_g, w_in_mix, lru_conv_w, lru_conv_b, lru_w_r, lru_b_r, lru_w_i,
              lru_b_i, lru_lambda, sgu_norm_g, sgu_w, sgu_b, w_out_mix,
              ffn2_norm_g, ffn2_w_in, ffn2_w_out, final_norm_g):
    bsz, n_lat, _ = x.shape
    rows = n_lat // GRID_W
    n_chunks_lat = rows // ROWS_PER_CHUNK
    n_chunks_ctx = CTX_LEN // CHUNK
    zeros_state = jnp.zeros((bsz, LRU_WIDTH), jnp.float32)

    sc = jax.nn.silu(c)
    scc = jax.nn.silu(c_ctx)[None, :]
    h, hc = x, ctx
    for l in range(DEPTH):
        last = l == DEPTH - 1
        m = jnp.split((sc @ w_ada[l] + b_ada[l])[:, None, :], N_MOD, axis=-1)
        mc = jnp.split((scc @ w_ada[l] + b_ada[l])[:, None, :], N_MOD, axis=-1)

        h = h + 0.5 * m[2] * swiglu(modulate(rmsnorm(h, ffn1_norm_g[l]), m[0], m[1]),
                                    ffn1_w_in[l], ffn1_w_out[l])
        hc = hc + 0.5 * mc[2] * swiglu(modulate(rmsnorm(hc, ffn1_norm_g[l]), mc[0], mc[1]),
                                       ffn1_w_in[l], ffn1_w_out[l])

        lru_p = (lru_conv_w[l], lru_conv_b[l], lru_w_r[l], lru_b_r[l],
                 lru_w_i[l], lru_b_i[l], lru_lambda[l])
        zc = modulate(rmsnorm(hc, mix_norm_g[l]), mc[3], mc[4])
        if not last:
            pc = zc @ w_in_mix[l]
            hf_c, hb_c = lru_scans(pc[..., :LRU_WIDTH], *lru_p, zeros_state, zeros_state)
            hc = hc + mc[5] * mix_out(pc, hf_c, hb_c, sgu_norm_g[l], sgu_w[l], sgu_b[l],
                                      w_out_mix[l], n_chunks_ctx)
        else:
            hf_c, hb_c = lru_scans(zc @ w_in_mix[l][:, :LRU_WIDTH], *lru_p,
                                   zeros_state, zeros_state)
        z = modulate(rmsnorm(h, mix_norm_g[l]), m[3], m[4])
        p = z @ w_in_mix[l]
        hf, hb = lru_scans(p[..., :LRU_WIDTH], *lru_p, hf_c[:, -1], hb_c[:, 0])
        h = h + m[5] * mix_out(p, hf, hb, sgu_norm_g[l], sgu_w[l], sgu_b[l],
                               w_out_mix[l], n_chunks_lat)

        h = h + 0.5 * m[8] * swiglu(modulate(rmsnorm(h, ffn2_norm_g[l]), m[6], m[7]),
                                    ffn2_w_in[l], ffn2_w_out[l])
        if not last:
            hc = hc + 0.5 * mc[8] * swiglu(modulate(rmsnorm(hc, ffn2_norm_g[l]), mc[6], mc[7]),
                                           ffn2_w_in[l], ffn2_w_out[l])

    return rmsnorm(h, final_norm_g)
```

```python
import functools

import jax
import jax.numpy as jnp
from jax import lax
from jax.experimental import pallas as pl
from jax.experimental.pallas import tpu as pltpu

F32 = jnp.float32
BF16 = jnp.bfloat16

EPS = 1e-6
D_MODEL = 1024
D_FF = 2816
LRU_WIDTH = 512
LRU_HEADS = 8
LRU_HEAD_DIM = 64
RG_C = 8.0
MLP_GROUPS = 4
MLP_GROUP_DIM = 128
MLP_WIDTH = 512
CHUNK = 128
IN_PROJ_WIDTH = 2048
N_MOD = 9
MOD_ROWS = 8
SUBLANES = 8
HALF = 256

VMEM_LIMIT = 56 * 1024 * 1024


def _cparams(n_axes):
    return pltpu.CompilerParams(
        dimension_semantics=("arbitrary",) * n_axes,
        vmem_limit_bytes=VMEM_LIMIT,
    )


def _resident(shape):
    nd = len(shape)
    return pl.BlockSpec(shape, lambda *_: (0,) * nd, pipeline_mode=pl.Buffered(1))


def _sigmoid(x):
    return 1.0 / (1.0 + jnp.exp(-x))


def _norm_mod(x, g, shift, scale):
    ms = jnp.mean(x * x, axis=-1, keepdims=True)
    return (x * lax.rsqrt(ms + EPS)) * (g * (1.0 + scale)) + shift


def _ada_kernel(c_ref, w_ref, b_ref, o_ref):
    c = c_ref[...]
    sc = c * _sigmoid(c)
    o_ref[0] = jnp.dot(sc, w_ref[0], preferred_element_type=F32,
                       precision=lax.Precision.HIGHEST) + b_ref[0]


def _ada(c_rows, w_ada, b_ada):
    depth = w_ada.shape[0]
    return pl.pallas_call(
        _ada_kernel,
        grid=(depth, N_MOD),
        in_specs=[
            pl.BlockSpec((MOD_ROWS, D_MODEL), lambda l, j: (0, 0)),
            pl.BlockSpec((1, D_MODEL, D_MODEL), lambda l, j: (l, 0, j)),
            pl.BlockSpec((1, 1, D_MODEL), lambda l, j: (l, 0, j)),
        ],
        out_specs=pl.BlockSpec((1, MOD_ROWS, D_MODEL), lambda l, j: (l, 0, j)),
        out_shape=jax.ShapeDtypeStruct((depth, MOD_ROWS, N_MOD * D_MODEL), F32),
        compiler_params=_cparams(2),
        name="ada_mod",
    )(c_rows, w_ada, b_ada.reshape(depth, 1, N_MOD * D_MODEL))


def _ffn_kernel(x_ref, mod_ref, g_ref, win_ref, wout_ref, fg_ref, o_ref, act_ref,
                *, k0, chunks, final):
    x = x_ref[...]
    shift = mod_ref[0, k0:k0 + 1, :]
    scale = mod_ref[0, k0 + 1:k0 + 2, :]
    gate = mod_ref[0, k0 + 2:k0 + 3, :]
    zb = _norm_mod(x, g_ref[...], shift, scale).astype(BF16)
    for c0, cw in chunks:
        g = jnp.dot(zb, win_ref[:, c0:c0 + cw], preferred_element_type=F32)
        u = jnp.dot(zb, win_ref[:, D_FF + c0:D_FF + c0 + cw], preferred_element_type=F32)
        act_ref[:, c0:c0 + cw] = (g * _sigmoid(g) * u).astype(BF16)
    y = jnp.dot(act_ref[...], wout_ref[...], preferred_element_type=F32)
    out = x + (0.5 * gate) * y
    if final:
        ms = jnp.mean(out * out, axis=-1, keepdims=True)
        out = out * lax.rsqrt(ms + EPS) * fg_ref[...]
    o_ref[...] = out


def _ffn_chunks(width):
    chunks, c0 = [], 0
    while c0 < D_FF:
        cw = min(width, D_FF - c0)
        chunks.append((c0, cw))
        c0 += cw
    return tuple(chunks)


def _ffn(x, mod, k0, norm_g, w_in, w_out, final_g=None, *, tm=512):
    n = x.shape[0]
    nb = mod.shape[0]
    tiles_per_mod = n // nb // tm
    final = final_g is not None
    fg = final_g if final else norm_g
    kern = functools.partial(_ffn_kernel, k0=k0, chunks=_ffn_chunks(512), final=final)
    return pl.pallas_call(
        kern,
        grid=(n // tm,),
        in_specs=[
            pl.BlockSpec((tm, D_MODEL), lambda i: (i, 0)),
            pl.BlockSpec((1, N_MOD, D_MODEL), lambda i: (i // tiles_per_mod, 0, 0)),
            _resident((1, D_MODEL)),
            _resident((D_MODEL, 2 * D_FF)),
            _resident((D_FF, D_MODEL)),
            _resident((1, D_MODEL)),
        ],
        out_specs=pl.BlockSpec((tm, D_MODEL), lambda i: (i, 0)),
        out_shape=jax.ShapeDtypeStruct((n, D_MODEL), F32),
        scratch_shapes=[pltpu.VMEM((tm, D_FF), BF16)],
        compiler_params=_cparams(1),
        name="ffn_final" if final else "ffn",
    )(x, mod, norm_g.reshape(1, D_MODEL), w_in, w_out, fg.reshape(1, D_MODEL))


def _inproj_kernel(x_ref, mod_ref, g_ref, w_ref, o_ref):
    shift = mod_ref[0, 3:4, :]
    scale = mod_ref[0, 4:5, :]
    zb = _norm_mod(x_ref[...], g_ref[...], shift, scale).astype(BF16)
    o_ref[...] = jnp.dot(zb, w_ref[...], preferred_element_type=F32)


def _inproj(x, mod, norm_g, w, *, tm=512):
    n = x.shape[0]
    nb = mod.shape[0]
    tiles_per_mod = n // nb // tm
    return pl.pallas_call(
        _inproj_kernel,
        grid=(n // tm,),
        in_specs=[
            pl.BlockSpec((tm, D_MODEL), lambda i: (i, 0)),
            pl.BlockSpec((1, N_MOD, D_MODEL), lambda i: (i // tiles_per_mod, 0, 0)),
            _resident((1, D_MODEL)),
            _resident((D_MODEL, IN_PROJ_WIDTH)),
        ],
        out_specs=pl.BlockSpec((tm, IN_PROJ_WIDTH), lambda i: (i, 0)),
        out_shape=jax.ShapeDtypeStruct((n, IN_PROJ_WIDTH), F32),
        compiler_params=_cparams(1),
        name="mix_inproj",
    )(x, mod, norm_g.reshape(1, D_MODEL), w)


def _softplus(x):
    return jnp.maximum(x, 0.0) + jnp.log1p(jnp.exp(-jnp.abs(x)))


def _lru_kernel(xf_ref, xfp_ref, xfn_ref, xb_ref, xbp_ref, xbn_ref,
                cw_ref, cb_ref, wg_ref, br_ref, bi_ref, lam_ref, h0_ref,
                hf_ref, hb_ref, xe_ref, a_ref, b_ref, carry_ref, *, ts, nt):
    j = pl.program_id(1)
    groups = ts // SUBLANES
    sub = lax.broadcasted_iota(jnp.int32, (groups, SUBLANES, HALF), 1)

    @pl.when(j == 0)
    def _():
        carry_ref[0] = jnp.broadcast_to(h0_ref[0, 0:1, :], (SUBLANES, LRU_WIDTH))
        carry_ref[1] = jnp.broadcast_to(h0_ref[0, 1:2, :], (SUBLANES, LRU_WIDTH))

    def coeffs(d, x_ref, xp_ref, xn_ref, tile):
        xe_ref[0:SUBLANES] = jnp.where(tile > 0, xp_ref[...], 0.0)
        xe_ref[SUBLANES:SUBLANES + ts] = x_ref[...]
        xe_ref[SUBLANES + ts:] = jnp.where(tile < nt - 1, xn_ref[...], 0.0)
        xc = cb_ref[...]
        for k in range(4):
            xc = xc + xe_ref[SUBLANES - 2 + k:SUBLANES - 2 + k + ts] * cw_ref[k:k + 1, :]
        xcb = xc.astype(BF16)
        neg_c_sp = -RG_C * _softplus(-lam_ref[d:d + 1, :])
        for hh in range(LRU_WIDTH // HALF):
            cols = slice(hh * HALF, (hh + 1) * HALF)
            pre = jnp.dot(xcb[:, cols], wg_ref[d, hh], preferred_element_type=F32)
            r = _sigmoid(pre[:, :HALF] + br_ref[d:d + 1, cols])
            i = _sigmoid(pre[:, HALF:] + bi_ref[d:d + 1, cols])
            log_a = neg_c_sp[:, cols] * r
            a = jnp.exp(log_a)
            t = jnp.tanh(log_a)
            bc = jnp.sqrt((-2.0 * t) / (1.0 - t)) * (i * xc[:, cols])
            a3 = a.reshape(groups, SUBLANES, HALF)
            b3 = bc.reshape(groups, SUBLANES, HALF)
            for s in (1, 2, 4):
                if d == 0:
                    sh, m = s, sub >= s
                else:
                    sh, m = SUBLANES - s, sub < SUBLANES - s
                ar = pltpu.roll(a3, sh, 1)
                brl = pltpu.roll(b3, sh, 1)
                b3 = b3 + a3 * jnp.where(m, brl, 0.0)
                a3 = a3 * jnp.where(m, ar, 1.0)
            a_ref[d, :, :, cols] = a3
            b_ref[d, :, :, cols] = b3

    jb = nt - 1 - j
    coeffs(0, xf_ref, xfp_ref, xfn_ref, j)
    coeffs(1, xb_ref, xbp_ref, xbn_ref, jb)

    def body(g, carry):
        cf, cb = carry
        hf = a_ref[0, g] * cf + b_ref[0, g]
        hf_ref[g] = hf
        gb = groups - 1 - g
        hb = a_ref[1, gb] * cb + b_ref[1, gb]
        hb_ref[gb] = hb
        cf = jnp.broadcast_to(hf[SUBLANES - 1:SUBLANES, :], (SUBLANES, LRU_WIDTH))
        cb = jnp.broadcast_to(hb[0:1, :], (SUBLANES, LRU_WIDTH))
        return cf, cb

    cf, cb = lax.fori_loop(0, groups, body, (carry_ref[0], carry_ref[1]), unroll=4)
    carry_ref[0] = cf
    carry_ref[1] = cb


def _lru(p, seq, conv_w, conv_b, wg, b_r, b_i, lam, h0, *, ts=256):
    n = p.shape[0]
    bsz = n // seq
    nt = seq // ts
    groups = ts // SUBLANES
    rb = ts // SUBLANES
    sb = seq // SUBLANES

    def tile_f(b, j):
        return (b * nt + j, 0)

    def tile_b(b, j):
        return (b * nt + nt - 1 - j, 0)

    def prev_f(b, j):
        return (b * sb + jnp.maximum(j * rb - 1, 0), 0)

    def next_f(b, j):
        return (b * sb + jnp.minimum((j + 1) * rb, sb - 1), 0)

    def prev_b(b, j):
        return (b * sb + jnp.maximum((nt - 1 - j) * rb - 1, 0), 0)

    def next_b(b, j):
        return (b * sb + jnp.minimum((nt - j) * rb, sb - 1), 0)

    kern = functools.partial(_lru_kernel, ts=ts, nt=nt)
    out_sds = jax.ShapeDtypeStruct((n // SUBLANES, SUBLANES, LRU_WIDTH), F32)
    hf, hb = pl.pallas_call(
        kern,
        grid=(bsz, nt),
        in_specs=[
            pl.BlockSpec((ts, LRU_WIDTH), tile_f),
            pl.BlockSpec((SUBLANES, LRU_WIDTH), prev_f),
            pl.BlockSpec((SUBLANES, LRU_WIDTH), next_f),
            pl.BlockSpec((ts, LRU_WIDTH), tile_b),
            pl.BlockSpec((SUBLANES, LRU_WIDTH), prev_b),
            pl.BlockSpec((SUBLANES, LRU_WIDTH), next_b),
            _resident((4, LRU_WIDTH)),
            _resident((1, LRU_WIDTH)),
            _resident((2, LRU_WIDTH // HALF, HALF, 2 * HALF)),
            _resident((2, LRU_WIDTH)),
            _resident((2, LRU_WIDTH)),
            _resident((2, LRU_WIDTH)),
            pl.BlockSpec((1, 2, LRU_WIDTH), lambda b, j: (b, 0, 0)),
        ],
        out_specs=[
            pl.BlockSpec((groups, SUBLANES, LRU_WIDTH), lambda b, j: (b * nt + j, 0, 0)),
            pl.BlockSpec((groups, SUBLANES, LRU_WIDTH), lambda b, j: (b * nt + nt - 1 - j, 0, 0)),
        ],
        out_shape=[out_sds, out_sds],
        scratch_shapes=[
            pltpu.VMEM((ts + 2 * SUBLANES, LRU_WIDTH), F32),
            pltpu.VMEM((2, groups, SUBLANES, LRU_WIDTH), F32),
            pltpu.VMEM((2, groups, SUBLANES, LRU_WIDTH), F32),
            pltpu.VMEM((2, SUBLANES, LRU_WIDTH), F32),
        ],
        compiler_params=_cparams(2),
        name="lru_scan",
    )(p, p, p, p, p, p, conv_w, conv_b.reshape(1, LRU_WIDTH), wg, b_r, b_i, lam, h0)
    return hf.reshape(n, LRU_WIDTH), hb.reshape(n, LRU_WIDTH)


def _gelu_tanh(x):
    return 0.5 * x * (1.0 + jnp.tanh(0.7978845608028654 * (x + 0.044715 * (x * x * x))))


def _mixout_kernel(x_ref, gl_ref, u_ref, v_ref, hf_ref, hb_ref, mod_ref,
                   sg_ref, sw_ref, sb_ref, wo_ref, o_ref, y_ref, *, tm):
    y_ref[:, 0:LRU_WIDTH] = ((hf_ref[...] + hb_ref[...]) * _gelu_tanh(gl_ref[...])).astype(BF16)
    v = v_ref[...]
    vc = v - jnp.mean(v, axis=-1, keepdims=True)
    var = jnp.mean(vc * vc, axis=-1, keepdims=True)
    vnb = (vc * lax.rsqrt(var + EPS) * sg_ref[...]).astype(BF16)
    for ch in range(tm // CHUNK):
        rows = slice(ch * CHUNK, (ch + 1) * CHUNK)
        for g in range(MLP_GROUPS):
            cols = slice(g * MLP_GROUP_DIM, (g + 1) * MLP_GROUP_DIM)
            z = jnp.dot(sw_ref[g], vnb[rows, cols], preferred_element_type=F32) + sb_ref[:, cols]
            y_ref[rows, LRU_WIDTH + g * MLP_GROUP_DIM:LRU_WIDTH + (g + 1) * MLP_GROUP_DIM] = (
                u_ref[rows, cols] * z).astype(BF16)
    acc = jnp.dot(y_ref[...], wo_ref[...], preferred_element_type=F32)
    o_ref[...] = x_ref[...] + mod_ref[0, 5:6, :] * acc


def _mixout(x, p, hf, hb, mod, sgu_g, sgu_w, sgu_bias, w_out, *, tm=512):
    n = x.shape[0]
    nb = mod.shape[0]
    tiles_per_mod = n // nb // tm
    kern = functools.partial(_mixout_kernel, tm=tm)
    return pl.pallas_call(
        kern,
        grid=(n // tm,),
        in_specs=[
            pl.BlockSpec((tm, D_MODEL), lambda i: (i, 0)),
            pl.BlockSpec((tm, LRU_WIDTH), lambda i: (i, 1)),
            pl.BlockSpec((tm, MLP_WIDTH), lambda i: (i, 2)),
            pl.BlockSpec((tm, MLP_WIDTH), lambda i: (i, 3)),
            pl.BlockSpec((tm, LRU_WIDTH), lambda i: (i, 0)),
            pl.BlockSpec((tm, LRU_WIDTH), lambda i: (i, 0)),
            pl.BlockSpec((1, N_MOD, D_MODEL), lambda i: (i // tiles_per_mod, 0, 0)),
            _resident((1, MLP_WIDTH)),
            _resident((MLP_GROUPS, CHUNK, CHUNK)),
            _resident((CHUNK, MLP_WIDTH)),
            _resident((D_MODEL, D_MODEL)),
        ],
        out_specs=pl.BlockSpec((tm, D_MODEL), lambda i: (i, 0)),
        out_shape=jax.ShapeDtypeStruct((n, D_MODEL), F32),
        scratch_shapes=[pltpu.VMEM((tm, D_MODEL), BF16)],
        compiler_params=_cparams(1),
        name="mix_out",
    )(x, p, p, p, hf, hb, mod, sgu_g.reshape(1, MLP_WIDTH), sgu_w, sgu_bias, w_out)


def _gate_weights(w_r, w_i):
    hpb = HALF // LRU_HEAD_DIM
    out = []
    for d in range(2):
        halves = []
        for hh in range(LRU_WIDTH // HALF):
            blocks_r = [w_r[d, hh * hpb + k] for k in range(hpb)]
            blocks_i = [w_i[d, hh * hpb + k] for k in range(hpb)]
            halves.append(jnp.concatenate(
                [jax.scipy.linalg.block_diag(*blocks_r), jax.scipy.linalg.block_diag(*blocks_i)],
                axis=1))
        out.append(jnp.stack(halves))
    return jnp.stack(out).astype(BF16)


def kernel(x, c, ctx, c_ctx, w_ada, b_ada, ffn1_norm_g, ffn1_w_in, ffn1_w_out,
           mix_norm_g, w_in_mix, lru_conv_w, lru_conv_b, lru_w_r, lru_b_r, lru_w_i,
           lru_b_i, lru_lambda, sgu_norm_g, sgu_w, sgu_b, w_out_mix,
           ffn2_norm_g, ffn2_w_in, ffn2_w_out, final_norm_g):
    bsz, n_lat, d = x.shape
    n_ctx = ctx.shape[1]
    depth = w_ada.shape[0]
    assert bsz + 1 <= MOD_ROWS and d == D_MODEL

    c_rows = jnp.zeros((MOD_ROWS, d), F32).at[:bsz].set(c).at[bsz].set(c_ctx)
    mods = _ada(c_rows, w_ada, b_ada).reshape(depth, MOD_ROWS, N_MOD, d)

    h = x.reshape(bsz * n_lat, d)
    hc = ctx.reshape(bsz * n_ctx, d)
    zeros_state = jnp.zeros((bsz, 2, LRU_WIDTH), F32)

    for l in range(depth):
        last = l == depth - 1
        m = mods[l, :bsz]
        mc = mods[l, bsz:bsz + 1]
        w1_in, w1_out = ffn1_w_in[l].astype(BF16), ffn1_w_out[l].astype(BF16)
        w2_in, w2_out = ffn2_w_in[l].astype(BF16), ffn2_w_out[l].astype(BF16)
        w_mix_in, w_mix_out = w_in_mix[l].astype(BF16), w_out_mix[l].astype(BF16)
        wg = _gate_weights(lru_w_r[l], lru_w_i[l])
        s_w = sgu_w[l].astype(BF16)
        s_bias = jnp.repeat(sgu_b[l].T, MLP_GROUP_DIM, axis=1)
        lru_args = (lru_conv_w[l], lru_conv_b[l], wg, lru_b_r[l], lru_b_i[l], lru_lambda[l])

        h = _ffn(h, m, 0, ffn1_norm_g[l], w1_in, w1_out)
        hc = _ffn(hc, mc, 0, ffn1_norm_g[l], w1_in, w1_out, tm=256)

        pc = _inproj(hc, mc, mix_norm_g[l], w_mix_in, tm=256)
        hf_c, hb_c = _lru(pc, n_ctx, *lru_args, zeros_state, ts=n_ctx)
        if not last:
            hc = _mixout(hc, pc, hf_c, hb_c, mc, sgu_norm_g[l], s_w, s_bias, w_mix_out, tm=256)
        h0 = jnp.stack([hf_c.reshape(bsz, n_ctx, LRU_WIDTH)[:, -1],
                        hb_c.reshape(bsz, n_ctx, LRU_WIDTH)[:, 0]], axis=1)

        p = _inproj(h, m, mix_norm_g[l], w_mix_in)
        hf, hb = _lru(p, n_lat, *lru_args, h0)
        h = _mixout(h, p, hf, hb, m, sgu_norm_g[l], s_w, s_bias, w_mix_out)

        h = _ffn(h, m, 6, ffn2_norm_g[l], w2_in, w2_out, final_norm_g if last else None)
        if not last:
            hc = _ffn(hc, mc, 6, ffn2_norm_g[l], w2_in, w2_out, tm=256)

    return h.reshape(bsz, n_lat, d)
```

```python
import functools

import jax
import jax.numpy as jnp
from jax import lax
from jax.experimental import pallas as pl
from jax.experimental.pallas import tpu as pltpu

F32 = jnp.float32
BF16 = jnp.bfloat16

EPS = 1e-6
D_MODEL = 1024
D_FF = 2816
LRU_WIDTH = 512
LRU_HEADS = 8
LRU_HEAD_DIM = 64
RG_C = 8.0
MLP_GROUPS = 4
MLP_GROUP_DIM = 128
MLP_WIDTH = 512
CHUNK = 128
IN_PROJ_WIDTH = 2048
N_MOD = 9
MOD_ROWS = 8
SUBLANES = 8
HALF = 256
TINY = 1e-30

VMEM_LIMIT = 56 * 1024 * 1024


def _cparams(n_axes):
    return pltpu.CompilerParams(
        dimension_semantics=("arbitrary",) * n_axes,
        vmem_limit_bytes=VMEM_LIMIT,
    )


def _resident(shape):
    nd = len(shape)
    return pl.BlockSpec(shape, lambda *_: (0,) * nd, pipeline_mode=pl.Buffered(1))


def _sigmoid(x):
    return 1.0 / (1.0 + jnp.exp(-x))


def _norm_mod(x, g, shift, scale):
    ms = jnp.mean(x * x, axis=-1, keepdims=True)
    return (x * lax.rsqrt(ms + EPS)) * (g * (1.0 + scale)) + shift


def _ada_kernel(c_ref, w_ref, b_ref, o_ref):
    c = c_ref[...]
    sc = c * _sigmoid(c)
    o_ref[0] = jnp.dot(sc, w_ref[0], preferred_element_type=F32,
                       precision=lax.Precision.HIGHEST) + b_ref[0]


def _ada(c_rows, w_ada, b_ada):
    depth = w_ada.shape[0]
    return pl.pallas_call(
        _ada_kernel,
        grid=(depth, N_MOD),
        in_specs=[
            pl.BlockSpec((MOD_ROWS, D_MODEL), lambda l, j: (0, 0)),
            pl.BlockSpec((1, D_MODEL, D_MODEL), lambda l, j: (l, 0, j)),
            pl.BlockSpec((1, 1, D_MODEL), lambda l, j: (l, 0, j)),
        ],
        out_specs=pl.BlockSpec((1, MOD_ROWS, D_MODEL), lambda l, j: (l, 0, j)),
        out_shape=jax.ShapeDtypeStruct((depth, MOD_ROWS, N_MOD * D_MODEL), F32),
        compiler_params=_cparams(2),
        name="ada_mod",
    )(c_rows, w_ada, b_ada.reshape(depth, 1, N_MOD * D_MODEL))


def _ffn_kernel(x_ref, mod_ref, g_ref, win_ref, wout_ref, fg_ref, o_ref, act_ref,
                *, k0, chunks, final):
    x = x_ref[...]
    shift = mod_ref[0, k0:k0 + 1, :]
    scale = mod_ref[0, k0 + 1:k0 + 2, :]
    gate = mod_ref[0, k0 + 2:k0 + 3, :]
    zb = _norm_mod(x, g_ref[...], shift, scale).astype(BF16)
    for c0, cw in chunks:
        g = jnp.dot(zb, win_ref[:, c0:c0 + cw], preferred_element_type=F32)
        u = jnp.dot(zb, win_ref[:, D_FF + c0:D_FF + c0 + cw], preferred_element_type=F32)
        act_ref[:, c0:c0 + cw] = (g * _sigmoid(g) * u).astype(BF16)
    y = jnp.dot(act_ref[...], wout_ref[...], preferred_element_type=F32)
    out = x + (0.5 * gate) * y
    if final:
        ms = jnp.mean(out * out, axis=-1, keepdims=True)
        out = out * lax.rsqrt(ms + EPS) * fg_ref[...]
    o_ref[...] = out


def _ffn_chunks(width):
    chunks, c0 = [], 0
    while c0 < D_FF:
        cw = min(width, D_FF - c0)
        chunks.append((c0, cw))
        c0 += cw
    return tuple(chunks)


def _ffn(x, mod, k0, norm_g, w_in, w_out, final_g=None, *, tm=512):
    n = x.shape[0]
    nb = mod.shape[0]
    tiles_per_mod = n // nb // tm
    final = final_g is not None
    fg = final_g if final else norm_g
    kern = functools.partial(_ffn_kernel, k0=k0, chunks=_ffn_chunks(512), final=final)
    return pl.pallas_call(
        kern,
        grid=(n // tm,),
        in_specs=[
            pl.BlockSpec((tm, D_MODEL), lambda i: (i, 0)),
            pl.BlockSpec((1, N_MOD, D_MODEL), lambda i: (i // tiles_per_mod, 0, 0)),
            _resident((1, D_MODEL)),
            _resident((D_MODEL, 2 * D_FF)),
            _resident((D_FF, D_MODEL)),
            _resident((1, D_MODEL)),
        ],
        out_specs=pl.BlockSpec((tm, D_MODEL), lambda i: (i, 0)),
        out_shape=jax.ShapeDtypeStruct((n, D_MODEL), F32),
        scratch_shapes=[pltpu.VMEM((tm, D_FF), BF16)],
        compiler_params=_cparams(1),
        name="ffn_final" if final else "ffn",
    )(x, mod, norm_g.reshape(1, D_MODEL), w_in, w_out, fg.reshape(1, D_MODEL))


def _softplus(x):
    return jnp.maximum(x, 0.0) + jnp.log1p(jnp.exp(-jnp.abs(x)))


def _gelu_tanh(x):
    return 0.5 * x * (1.0 + jnp.tanh(0.7978845608028654 * (x + 0.044715 * (x * x * x))))


def _lru_input(x_ref, xp_ref, xn_ref, mod_ref, g_ref, wxl, xe_ref, tile, nt, ts):
    shift = mod_ref[0, 3:4, :]
    scale = mod_ref[0, 4:5, :]
    g = g_ref[...]
    zb = _norm_mod(x_ref[...], g, shift, scale).astype(BF16)
    halo = jnp.concatenate([xp_ref[...], xn_ref[...]], axis=0)
    hb = _norm_mod(halo, g, shift, scale).astype(BF16)
    xl_halo = jnp.dot(hb, wxl, preferred_element_type=F32)
    xe_ref[0:SUBLANES] = jnp.where(tile > 0, xl_halo[0:SUBLANES], 0.0)
    xe_ref[SUBLANES:SUBLANES + ts] = jnp.dot(zb, wxl, preferred_element_type=F32)
    xe_ref[SUBLANES + ts:] = jnp.where(tile < nt - 1, xl_halo[SUBLANES:], 0.0)
    return zb


def _short_conv(xe_ref, cw_ref, cb_ref, ts):
    xe = xe_ref[...]
    rows = ts + 2 * SUBLANES
    xc = cb_ref[...] + xe[SUBLANES:SUBLANES + ts] * cw_ref[2:3, :]
    for k, sh in ((0, 2), (1, 1), (3, rows - 1)):
        xc = xc + pltpu.roll(xe, sh, 0)[SUBLANES:SUBLANES + ts] * cw_ref[k:k + 1, :]
    return xc


def _scan_coeffs(d, xc, wg_ref, br_ref, bi_ref, lam_ref, a_ref, b_ref, ts):
    groups = ts // SUBLANES
    sub = lax.broadcasted_iota(jnp.int32, (groups, SUBLANES, HALF), 1)
    xcb = xc.astype(BF16)
    half_k = (-0.5 * RG_C) * _softplus(-lam_ref[d:d + 1, :])
    for hh in range(LRU_WIDTH // HALF):
        cols = slice(hh * HALF, (hh + 1) * HALF)
        pre = jnp.dot(xcb[:, cols], wg_ref[d, hh], preferred_element_type=F32)
        tr = jnp.tanh(pre[:, :HALF] + 0.5 * br_ref[d:d + 1, cols])
        ti = jnp.tanh(pre[:, HALF:] + 0.5 * bi_ref[d:d + 1, cols])
        kk = half_k[:, cols]
        log_a = kk + kk * tr
        a = jnp.exp(log_a)
        t = jnp.tanh(log_a)
        q = t / (t - 1.0)
        gain = q * lax.rsqrt(jnp.maximum(q, TINY))
        bc = gain * ((0.7071067811865476 * xc[:, cols]) * (1.0 + ti))
        a3 = a.reshape(groups, SUBLANES, HALF)
        b3 = bc.reshape(groups, SUBLANES, HALF)
        for s in (1, 2, 4):
            if d == 0:
                sh, m = s, sub >= s
            else:
                sh, m = SUBLANES - s, sub < SUBLANES - s
            am = jnp.where(m, a3, 0.0)
            b3 = b3 + am * pltpu.roll(b3, sh, 1)
            a3 = jnp.where(m, a3 * pltpu.roll(a3, sh, 1), a3)
        a_ref[:, :, cols] = a3
        b_ref[:, :, cols] = b3


def _carry_scan(d, a_ref, b_ref, h_ref, carry, groups):
    def body(i, c):
        g = i if d == 0 else groups - 1 - i
        h = a_ref[g] * c + b_ref[g]
        h_ref[g] = h
        row = h[SUBLANES - 1:SUBLANES, :] if d == 0 else h[0:1, :]
        return jnp.broadcast_to(row, (SUBLANES, LRU_WIDTH))

    return lax.fori_loop(0, groups, body, carry, unroll=8)


def _mix_bwd_kernel(x_ref, xp_ref, xn_ref, mod_ref, g_ref, wxl_ref, cw_ref, cb_ref,
                    wg_ref, br_ref, bi_ref, lam_ref, h0_ref,
                    hb_ref, xe_ref, a_ref, b_ref, carry_ref, *, ts, nt):
    j = pl.program_id(1)
    tile = nt - 1 - j
    groups = ts // SUBLANES

    @pl.when(j == 0)
    def _():
        carry_ref[...] = jnp.broadcast_to(h0_ref[0], (SUBLANES, LRU_WIDTH))

    _lru_input(x_ref, xp_ref, xn_ref, mod_ref, g_ref, wxl_ref[...], xe_ref, tile, nt, ts)
    xc = _short_conv(xe_ref, cw_ref, cb_ref, ts)
    _scan_coeffs(1, xc, wg_ref, br_ref, bi_ref, lam_ref, a_ref, b_ref, ts)
    carry_ref[...] = _carry_scan(1, a_ref, b_ref, hb_ref, carry_ref[...], groups)


def _mix_fwd_kernel(x_ref, xp_ref, xn_ref, hb_ref, mod_ref, g_ref, win_ref, cw_ref, cb_ref,
                    wg_ref, br_ref, bi_ref, lam_ref, h0_ref, sg_ref, sw_ref, sb_ref, wo_ref,
                    o_ref, hlast_ref, xe_ref, a_ref, b_ref, hf_ref, y_ref, carry_ref, *, ts, nt):
    j = pl.program_id(1)
    groups = ts // SUBLANES

    @pl.when(j == 0)
    def _():
        carry_ref[...] = jnp.broadcast_to(h0_ref[0], (SUBLANES, LRU_WIDTH))

    zb = _lru_input(x_ref, xp_ref, xn_ref, mod_ref, g_ref, win_ref[:, 0:LRU_WIDTH],
                    xe_ref, j, nt, ts)
    xc = _short_conv(xe_ref, cw_ref, cb_ref, ts)
    _scan_coeffs(0, xc, wg_ref, br_ref, bi_ref, lam_ref, a_ref, b_ref, ts)
    carry = _carry_scan(0, a_ref, b_ref, hf_ref, carry_ref[...], groups)
    carry_ref[...] = carry
    hlast_ref[0] = carry

    gl = jnp.dot(zb, win_ref[:, LRU_WIDTH:2 * LRU_WIDTH], preferred_element_type=F32)
    h_sum = hf_ref[...].reshape(ts, LRU_WIDTH) + hb_ref[...]
    y_ref[:, 0:LRU_WIDTH] = (h_sum * _gelu_tanh(gl)).astype(BF16)

    v = jnp.dot(zb, win_ref[:, 2 * LRU_WIDTH + MLP_WIDTH:], preferred_element_type=F32)
    vc = v - jnp.mean(v, axis=-1, keepdims=True)
    var = jnp.mean(vc * vc, axis=-1, keepdims=True)
    vnb = (vc * lax.rsqrt(var + EPS) * sg_ref[...]).astype(BF16)
    u = jnp.dot(zb, win_ref[:, 2 * LRU_WIDTH:2 * LRU_WIDTH + MLP_WIDTH], preferred_element_type=F32)
    for ch in range(ts // CHUNK):
        rows = slice(ch * CHUNK, (ch + 1) * CHUNK)
        for g in range(MLP_GROUPS):
            cols = slice(g * MLP_GROUP_DIM, (g + 1) * MLP_GROUP_DIM)
            z = jnp.dot(sw_ref[g], vnb[rows, cols], preferred_element_type=F32) + sb_ref[:, cols]
            y_ref[rows, LRU_WIDTH + g * MLP_GROUP_DIM:LRU_WIDTH + (g + 1) * MLP_GROUP_DIM] = (
                u[rows, cols] * z).astype(BF16)
    acc = jnp.dot(y_ref[...], wo_ref[...], preferred_element_type=F32)
    o_ref[...] = x_ref[...] + mod_ref[0, 5:6, :] * acc


def _tile_specs(seq, ts, reverse):
    nt = seq // ts
    rb = ts // SUBLANES
    sb = seq // SUBLANES

    def tidx(j):
        return nt - 1 - j if reverse else j

    return [
        pl.BlockSpec((ts, D_MODEL), lambda b, j: (b * nt + tidx(j), 0)),
        pl.BlockSpec((SUBLANES, D_MODEL),
                     lambda b, j: (b * sb + jnp.maximum(tidx(j) * rb - 1, 0), 0)),
        pl.BlockSpec((SUBLANES, D_MODEL),
                     lambda b, j: (b * sb + jnp.minimum((tidx(j) + 1) * rb, sb - 1), 0)),
    ]


def _mix(x, seq, mod, norm_g, w_in, lru, h0_f, h0_b, sgu_g, sgu_w, sgu_bias, w_out, *, ts):
    conv_w, conv_b, wg, b_r, b_i, lam = lru
    n = x.shape[0]
    bsz = n // seq
    nt = seq // ts
    groups = ts // SUBLANES
    shared_mod = mod.shape[0] == 1
    mod_spec = pl.BlockSpec((1, N_MOD, D_MODEL), lambda b, j: (0 if shared_mod else b, 0, 0))
    state_spec = pl.BlockSpec((1, 1, LRU_WIDTH), lambda b, j: (b, 0, 0))
    lru_specs = [
        _resident((4, LRU_WIDTH)),
        _resident((1, LRU_WIDTH)),
        _resident((2, LRU_WIDTH // HALF, HALF, 2 * HALF)),
        _resident((2, LRU_WIDTH)),
        _resident((2, LRU_WIDTH)),
        _resident((2, LRU_WIDTH)),
    ]
    lru_args = (conv_w, conv_b.reshape(1, LRU_WIDTH), wg, b_r, b_i, lam)
    g2 = norm_g.reshape(1, D_MODEL)
    scan_scratch = [
        pltpu.VMEM((ts + 2 * SUBLANES, LRU_WIDTH), F32),
        pltpu.VMEM((groups, SUBLANES, LRU_WIDTH), F32),
        pltpu.VMEM((groups, SUBLANES, LRU_WIDTH), F32),
    ]
    carry_scratch = pltpu.VMEM((SUBLANES, LRU_WIDTH), F32)

    hb = pl.pallas_call(
        functools.partial(_mix_bwd_kernel, ts=ts, nt=nt),
        grid=(bsz, nt),
        in_specs=_tile_specs(seq, ts, True) + [
            mod_spec, _resident((1, D_MODEL)), _resident((D_MODEL, LRU_WIDTH)),
        ] + lru_specs + [state_spec],
        out_specs=pl.BlockSpec((groups, SUBLANES, LRU_WIDTH),
                               lambda b, j: (b * nt + nt - 1 - j, 0, 0)),
        out_shape=jax.ShapeDtypeStruct((n // SUBLANES, SUBLANES, LRU_WIDTH), F32),
        scratch_shapes=scan_scratch + [carry_scratch],
        compiler_params=_cparams(2),
        name="mix_bwd",
    )(x, x, x, mod, g2, w_in[:, :LRU_WIDTH], *lru_args, h0_b)
    hb = hb.reshape(n, LRU_WIDTH)

    out, h_last = pl.pallas_call(
        functools.partial(_mix_fwd_kernel, ts=ts, nt=nt),
        grid=(bsz, nt),
        in_specs=_tile_specs(seq, ts, False) + [
            pl.BlockSpec((ts, LRU_WIDTH), lambda b, j: (b * nt + j, 0)),
            mod_spec, _resident((1, D_MODEL)), _resident((D_MODEL, IN_PROJ_WIDTH)),
        ] + lru_specs + [
            state_spec,
            _resident((1, MLP_WIDTH)),
            _resident((MLP_GROUPS, CHUNK, CHUNK)),
            _resident((CHUNK, MLP_WIDTH)),
            _resident((D_MODEL, D_MODEL)),
        ],
        out_specs=[
            pl.BlockSpec((ts, D_MODEL), lambda b, j: (b * nt + j, 0)),
            pl.BlockSpec((1, SUBLANES, LRU_WIDTH), lambda b, j: (b, 0, 0)),
        ],
        out_shape=[
            jax.ShapeDtypeStruct((n, D_MODEL), F32),
            jax.ShapeDtypeStruct((bsz, SUBLANES, LRU_WIDTH), F32),
        ],
        scratch_shapes=scan_scratch + [
            pltpu.VMEM((groups, SUBLANES, LRU_WIDTH), F32),
            pltpu.VMEM((ts, D_MODEL), BF16),
            carry_scratch,
        ],
        compiler_params=_cparams(2),
        name="mix_fwd",
    )(x, x, x, hb, mod, g2, w_in, *lru_args, h0_f,
      sgu_g.reshape(1, MLP_WIDTH), sgu_w, sgu_bias, w_out)
    return out, h_last[:, 0:1, :], hb


def _gate_weights(w_r, w_i):
    hpb = HALF // LRU_HEAD_DIM
    out = []
    for d in range(2):
        halves = []
        for hh in range(LRU_WIDTH // HALF):
            blocks_r = [w_r[d, hh * hpb + k] for k in range(hpb)]
            blocks_i = [w_i[d, hh * hpb + k] for k in range(hpb)]
            halves.append(jnp.concatenate(
                [jax.scipy.linalg.block_diag(*blocks_r), jax.scipy.linalg.block_diag(*blocks_i)],
                axis=1))
        out.append(jnp.stack(halves))
    return (0.5 * jnp.stack(out)).astype(BF16)


def kernel(x, c, ctx, c_ctx, w_ada, b_ada, ffn1_norm_g, ffn1_w_in, ffn1_w_out,
           mix_norm_g, w_in_mix, lru_conv_w, lru_conv_b, lru_w_r, lru_b_r, lru_w_i,
           lru_b_i, lru_lambda, sgu_norm_g, sgu_w, sgu_b, w_out_mix,
           ffn2_norm_g, ffn2_w_in, ffn2_w_out, final_norm_g):
    bsz, n_lat, d = x.shape
    n_ctx = ctx.shape[1]
    depth = w_ada.shape[0]
    assert bsz + 1 <= MOD_ROWS and d == D_MODEL

    c_rows = jnp.zeros((MOD_ROWS, d), F32).at[:bsz].set(c).at[bsz].set(c_ctx)
    mods = _ada(c_rows, w_ada, b_ada).reshape(depth, MOD_ROWS, N_MOD, d)

    h = x.reshape(bsz * n_lat, d)
    hc = ctx.reshape(bsz * n_ctx, d)
    zeros_state = jnp.zeros((bsz, 1, LRU_WIDTH), F32)

    for l in range(depth):
        last = l == depth - 1
        m = mods[l, :bsz]
        mc = mods[l, bsz:bsz + 1]
        w1_in, w1_out = ffn1_w_in[l].astype(BF16), ffn1_w_out[l].astype(BF16)
        w2_in, w2_out = ffn2_w_in[l].astype(BF16), ffn2_w_out[l].astype(BF16)
        w_mix_in, w_mix_out = w_in_mix[l].astype(BF16), w_out_mix[l].astype(BF16)
        lru = (lru_conv_w[l], lru_conv_b[l], _gate_weights(lru_w_r[l], lru_w_i[l]),
               lru_b_r[l], lru_b_i[l], lru_lambda[l])
        s_w = sgu_w[l].astype(BF16)
        s_bias = jnp.repeat(sgu_b[l].T, MLP_GROUP_DIM, axis=1)
        sgu = (sgu_norm_g[l], s_w, s_bias, w_mix_out)

        h = _ffn(h, m, 0, ffn1_norm_g[l], w1_in, w1_out)
        hc = _ffn(hc, mc, 0, ffn1_norm_g[l], w1_in, w1_out, tm=256)

        hc, hf_last, hb_c = _mix(hc, n_ctx, mc, mix_norm_g[l], w_mix_in, lru,
                                 zeros_state, zeros_state, *sgu, ts=n_ctx)
        h0_b = hb_c.reshape(bsz, n_ctx, LRU_WIDTH)[:, 0:1, :]
        h, _, _ = _mix(h, n_lat, m, mix_norm_g[l], w_mix_in, lru, hf_last, h0_b, *sgu, ts=512)

        h = _ffn(h, m, 6, ffn2_norm_g[l], w2_in, w2_out, final_norm_g if last else None)
        if not last:
            hc = _ffn(hc, mc, 6, ffn2_norm_g[l], w2_in, w2_out, tm=256)

    return h.reshape(bsz, n_lat, d)
```

```python
import functools

import jax
import jax.numpy as jnp
from jax import lax
from jax.experimental import pallas as pl
from jax.experimental.pallas import tpu as pltpu

F32 = jnp.float32
BF16 = jnp.bfloat16

EPS = 1e-6
D_MODEL = 1024
D_FF = 2816
LRU_WIDTH = 512
LRU_HEADS = 8
LRU_HEAD_DIM = 64
RG_C = 8.0
MLP_GROUPS = 4
MLP_GROUP_DIM = 128
MLP_WIDTH = 512
CHUNK = 128
IN_PROJ_WIDTH = 2048
N_MOD = 9
MOD_ROWS = 8
SUBLANES = 8
HALF = 256
HALO = 16
TINY = 1e-30

VMEM_LIMIT = 56 * 1024 * 1024


def _cparams(n_axes):
    return pltpu.CompilerParams(
        dimension_semantics=("arbitrary",) * n_axes,
        vmem_limit_bytes=VMEM_LIMIT,
    )


def _resident(shape):
    nd = len(shape)
    return pl.BlockSpec(shape, lambda *_: (0,) * nd, pipeline_mode=pl.Buffered(1))


def _sigmoid(x):
    return 1.0 / (1.0 + jnp.exp(-x))


def _norm_mod(x, g, shift, scale):
    ms = jnp.mean(x * x, axis=-1, keepdims=True)
    return (x * lax.rsqrt(ms + EPS)) * (g * (1.0 + scale)) + shift


def _ada_kernel(c_ref, w_ref, b_ref, o_ref):
    c = c_ref[...]
    sc = c * _sigmoid(c)
    o_ref[0] = jnp.dot(sc, w_ref[0], preferred_element_type=F32,
                       precision=lax.Precision.HIGHEST) + b_ref[0]


def _ada(c_rows, w_ada, b_ada):
    depth = w_ada.shape[0]
    return pl.pallas_call(
        _ada_kernel,
        grid=(depth, N_MOD),
        in_specs=[
            pl.BlockSpec((MOD_ROWS, D_MODEL), lambda l, j: (0, 0)),
            pl.BlockSpec((1, D_MODEL, D_MODEL), lambda l, j: (l, 0, j)),
            pl.BlockSpec((1, 1, D_MODEL), lambda l, j: (l, 0, j)),
        ],
        out_specs=pl.BlockSpec((1, MOD_ROWS, D_MODEL), lambda l, j: (l, 0, j)),
        out_shape=jax.ShapeDtypeStruct((depth, MOD_ROWS, N_MOD * D_MODEL), F32),
        compiler_params=_cparams(2),
        name="ada_mod",
    )(c_rows, w_ada, b_ada.reshape(depth, 1, N_MOD * D_MODEL))


def _ffn_kernel(x_ref, mod_ref, g_ref, win_ref, wout_ref, fg_ref, o_ref, act_ref,
                *, k0, chunks, final):
    x = x_ref[...]
    shift = mod_ref[0, k0:k0 + 1, :]
    scale = mod_ref[0, k0 + 1:k0 + 2, :]
    gate = mod_ref[0, k0 + 2:k0 + 3, :]
    zb = _norm_mod(x, g_ref[...], shift, scale).astype(BF16)
    for c0, cw in chunks:
        g = jnp.dot(zb, win_ref[:, c0:c0 + cw], preferred_element_type=F32)
        u = jnp.dot(zb, win_ref[:, D_FF + c0:D_FF + c0 + cw], preferred_element_type=F32)
        act_ref[:, c0:c0 + cw] = (g * _sigmoid(g) * u).astype(BF16)
    y = jnp.dot(act_ref[...], wout_ref[...], preferred_element_type=F32)
    out = x + (0.5 * gate) * y
    if final:
        ms = jnp.mean(out * out, axis=-1, keepdims=True)
        out = out * lax.rsqrt(ms + EPS) * fg_ref[...]
    o_ref[...] = out


def _ffn_chunks(width):
    chunks, c0 = [], 0
    while c0 < D_FF:
        cw = min(width, D_FF - c0)
        chunks.append((c0, cw))
        c0 += cw
    return tuple(chunks)


def _ffn(x, mod, k0, norm_g, w_in, w_out, final_g=None, *, tm=1024):
    n = x.shape[0]
    nb = mod.shape[0]
    tiles_per_mod = n // nb // tm
    final = final_g is not None
    fg = final_g if final else norm_g
    kern = functools.partial(_ffn_kernel, k0=k0, chunks=_ffn_chunks(512), final=final)
    return pl.pallas_call(
        kern,
        grid=(n // tm,),
        in_specs=[
            pl.BlockSpec((tm, D_MODEL), lambda i: (i, 0)),
            pl.BlockSpec((1, N_MOD, D_MODEL), lambda i: (i // tiles_per_mod, 0, 0)),
            _resident((1, D_MODEL)),
            _resident((D_MODEL, 2 * D_FF)),
            _resident((D_FF, D_MODEL)),
            _resident((1, D_MODEL)),
        ],
        out_specs=pl.BlockSpec((tm, D_MODEL), lambda i: (i, 0)),
        out_shape=jax.ShapeDtypeStruct((n, D_MODEL), F32),
        scratch_shapes=[pltpu.VMEM((tm, D_FF), BF16)],
        compiler_params=_cparams(1),
        name="ffn_final" if final else "ffn",
    )(x, mod, norm_g.reshape(1, D_MODEL), w_in, w_out, fg.reshape(1, D_MODEL))


def _softplus(x):
    return jnp.maximum(x, 0.0) + jnp.log1p(jnp.exp(-jnp.abs(x)))


def _gelu_tanh(x):
    return 0.5 * x * (1.0 + jnp.tanh(0.7978845608028654 * (x + 0.044715 * (x * x * x))))


def _lru_input(x_ref, xp_ref, xn_ref, mod_ref, g_ref, wxl_ref, zb_ref, xe_ref, tile, nt, ts):
    shift = mod_ref[0, 3:4, :]
    scale = mod_ref[0, 4:5, :]
    g = g_ref[...]
    zb_ref[0:HALO] = _norm_mod(xp_ref[...], g, shift, scale).astype(BF16)
    zb_ref[HALO:HALO + ts] = _norm_mod(x_ref[...], g, shift, scale).astype(BF16)
    zb_ref[HALO + ts:] = _norm_mod(xn_ref[...], g, shift, scale).astype(BF16)
    xe_ref[...] = jnp.dot(zb_ref[...], wxl_ref[:, 0:LRU_WIDTH], preferred_element_type=F32)
    xe_ref[0:HALO] = jnp.where(tile > 0, xe_ref[0:HALO], 0.0)
    xe_ref[HALO + ts:] = jnp.where(tile < nt - 1, xe_ref[HALO + ts:], 0.0)


def _short_conv(xe_ref, cw_ref, cb_ref, ts):
    xe = xe_ref[...]
    rows = ts + 2 * HALO
    xc = cb_ref[...] + xe[HALO:HALO + ts] * cw_ref[2:3, :]
    for k, sh in ((0, 2), (1, 1), (3, rows - 1)):
        xc = xc + pltpu.roll(xe, sh, 0)[HALO:HALO + ts] * cw_ref[k:k + 1, :]
    return xc


def _scan_coeffs(d, xc, wg_ref, br_ref, bi_ref, lam_ref, a_ref, b_ref, ts):
    groups = ts // SUBLANES
    sub = lax.broadcasted_iota(jnp.int32, (groups, SUBLANES, HALF), 1)
    xcb = xc.astype(BF16)
    half_k = (-0.5 * RG_C) * _softplus(-lam_ref[d:d + 1, :])
    for hh in range(LRU_WIDTH // HALF):
        cols = slice(hh * HALF, (hh + 1) * HALF)
        pre = jnp.dot(xcb[:, cols], wg_ref[d, hh], preferred_element_type=F32)
        tr = jnp.tanh(pre[:, :HALF] + 0.5 * br_ref[d:d + 1, cols])
        ti = jnp.tanh(pre[:, HALF:] + 0.5 * bi_ref[d:d + 1, cols])
        kk = half_k[:, cols]
        log_a = kk + kk * tr
        a = jnp.exp(log_a)
        t = jnp.tanh(log_a)
        q = t / (t - 1.0)
        gain = q * lax.rsqrt(jnp.maximum(q, TINY))
        bc = gain * ((0.7071067811865476 * xc[:, cols]) * (1.0 + ti))
        a3 = a.reshape(groups, SUBLANES, HALF)
        b3 = bc.reshape(groups, SUBLANES, HALF)
        for s in (1, 2, 4):
            if d == 0:
                sh, m = s, sub >= s
            else:
                sh, m = SUBLANES - s, sub < SUBLANES - s
            am = jnp.where(m, a3, 0.0)
            b3 = b3 + am * pltpu.roll(b3, sh, 1)
            a3 = jnp.where(m, a3 * pltpu.roll(a3, sh, 1), a3)
        a_ref[:, :, cols] = a3
        b_ref[:, :, cols] = b3


def _carry_scan(d, a_ref, b_ref, h_ref, carry, groups):
    def body(i, c):
        g = i if d == 0 else groups - 1 - i
        h = a_ref[g] * c + b_ref[g]
        h_ref[g] = h
        row = h[SUBLANES - 1:SUBLANES, :] if d == 0 else h[0:1, :]
        return jnp.broadcast_to(row, (SUBLANES, LRU_WIDTH))

    return lax.fori_loop(0, groups, body, carry, unroll=8)


def _mix_bwd_kernel(x_ref, xp_ref, xn_ref, mod_ref, g_ref, wxl_ref, cw_ref, cb_ref,
                    wg_ref, br_ref, bi_ref, lam_ref, h0_ref,
                    hb_ref, zb_ref, xe_ref, a_ref, b_ref, carry_ref, *, ts, nt):
    j = pl.program_id(1)
    tile = nt - 1 - j
    groups = ts // SUBLANES

    @pl.when(j == 0)
    def _():
        carry_ref[...] = jnp.broadcast_to(h0_ref[0], (SUBLANES, LRU_WIDTH))

    _lru_input(x_ref, xp_ref, xn_ref, mod_ref, g_ref, wxl_ref, zb_ref, xe_ref, tile, nt, ts)
    xc = _short_conv(xe_ref, cw_ref, cb_ref, ts)
    _scan_coeffs(1, xc, wg_ref, br_ref, bi_ref, lam_ref, a_ref, b_ref, ts)
    carry_ref[...] = _carry_scan(1, a_ref, b_ref, hb_ref, carry_ref[...], groups)


def _mix_fwd_kernel(x_ref, xp_ref, xn_ref, hb_ref, mod_ref, g_ref, win_ref, cw_ref, cb_ref,
                    wg_ref, br_ref, bi_ref, lam_ref, h0_ref, sg_ref, sw_ref, sb_ref, wo_ref,
                    o_ref, hlast_ref,
                    zb_ref, xe_ref, a_ref, b_ref, hf_ref, gg_ref, u_ref, vn_ref, y_ref, carry_ref,
                    *, ts, nt):
    j = pl.program_id(1)
    groups = ts // SUBLANES
    n_chunks = ts // CHUNK

    @pl.when(j == 0)
    def _():
        carry_ref[...] = jnp.broadcast_to(h0_ref[0], (SUBLANES, LRU_WIDTH))

    _lru_input(x_ref, xp_ref, xn_ref, mod_ref, g_ref, win_ref, zb_ref, xe_ref, j, nt, ts)
    zb = zb_ref[HALO:HALO + ts]
    gl = jnp.dot(zb, win_ref[:, LRU_WIDTH:2 * LRU_WIDTH], preferred_element_type=F32)
    gg_ref[...] = _gelu_tanh(gl)
    v = jnp.dot(zb, win_ref[:, 2 * LRU_WIDTH + MLP_WIDTH:], preferred_element_type=F32)
    vc = v - jnp.mean(v, axis=-1, keepdims=True)
    var = jnp.mean(vc * vc, axis=-1, keepdims=True)
    vn_ref[...] = (vc * lax.rsqrt(var + EPS) * sg_ref[...]).astype(BF16)
    u_ref[...] = jnp.dot(zb, win_ref[:, 2 * LRU_WIDTH:2 * LRU_WIDTH + MLP_WIDTH],
                         preferred_element_type=F32)
    xc = _short_conv(xe_ref, cw_ref, cb_ref, ts)
    _scan_coeffs(0, xc, wg_ref, br_ref, bi_ref, lam_ref, a_ref, b_ref, ts)

    carry = _carry_scan(0, a_ref, b_ref, hf_ref, carry_ref[...], groups)
    carry_ref[...] = carry
    hlast_ref[0] = carry

    h_sum = hf_ref[...].reshape(ts, LRU_WIDTH) + hb_ref[...]
    y_ref[:, 0:LRU_WIDTH] = (h_sum * gg_ref[...]).astype(BF16)
    for g in range(MLP_GROUPS):
        cols = slice(g * MLP_GROUP_DIM, (g + 1) * MLP_GROUP_DIM)
        rhs = jnp.concatenate(
            [vn_ref[ch * CHUNK:(ch + 1) * CHUNK, cols] for ch in range(n_chunks)], axis=1)
        z = jnp.dot(sw_ref[g], rhs, preferred_element_type=F32)
        for ch in range(n_chunks):
            rows = slice(ch * CHUNK, (ch + 1) * CHUNK)
            zc = z[:, ch * MLP_GROUP_DIM:(ch + 1) * MLP_GROUP_DIM] + sb_ref[:, cols]
            y_ref[rows, LRU_WIDTH + g * MLP_GROUP_DIM:LRU_WIDTH + (g + 1) * MLP_GROUP_DIM] = (
                u_ref[rows, cols] * zc).astype(BF16)
    acc = jnp.dot(y_ref[...], wo_ref[...], preferred_element_type=F32)
    o_ref[...] = x_ref[...] + mod_ref[0, 5:6, :] * acc


def _tile_specs(seq, ts, reverse):
    nt = seq // ts
    rb = ts // HALO
    sb = seq // HALO

    def tidx(j):
        return nt - 1 - j if reverse else j

    return [
        pl.BlockSpec((ts, D_MODEL), lambda b, j: (b * nt + tidx(j), 0)),
        pl.BlockSpec((HALO, D_MODEL),
                     lambda b, j: (b * sb + jnp.maximum(tidx(j) * rb - 1, 0), 0)),
        pl.BlockSpec((HALO, D_MODEL),
                     lambda b, j: (b * sb + jnp.minimum((tidx(j) + 1) * rb, sb - 1), 0)),
    ]


def _mix(x, seq, mod, norm_g, w_in, lru, h0_f, h0_b, sgu_g, sgu_w, sgu_bias, w_out, *, ts):
    conv_w, conv_b, wg, b_r, b_i, lam = lru
    n = x.shape[0]
    bsz = n // seq
    nt = seq // ts
    groups = ts // SUBLANES
    shared_mod = mod.shape[0] == 1
    mod_spec = pl.BlockSpec((1, N_MOD, D_MODEL), lambda b, j: (0 if shared_mod else b, 0, 0))
    state_spec = pl.BlockSpec((1, 1, LRU_WIDTH), lambda b, j: (b, 0, 0))
    lru_specs = [
        _resident((4, LRU_WIDTH)),
        _resident((1, LRU_WIDTH)),
        _resident((2, LRU_WIDTH // HALF, HALF, 2 * HALF)),
        _resident((2, LRU_WIDTH)),
        _resident((2, LRU_WIDTH)),
        _resident((2, LRU_WIDTH)),
    ]
    lru_args = (conv_w, conv_b.reshape(1, LRU_WIDTH), wg, b_r, b_i, lam)
    g2 = norm_g.reshape(1, D_MODEL)
    scan_scratch = [
        pltpu.VMEM((ts + 2 * HALO, D_MODEL), BF16),
        pltpu.VMEM((ts + 2 * HALO, LRU_WIDTH), F32),
        pltpu.VMEM((groups, SUBLANES, LRU_WIDTH), F32),
        pltpu.VMEM((groups, SUBLANES, LRU_WIDTH), F32),
    ]
    carry_scratch = pltpu.VMEM((SUBLANES, LRU_WIDTH), F32)

    hb = pl.pallas_call(
        functools.partial(_mix_bwd_kernel, ts=ts, nt=nt),
        grid=(bsz, nt),
        in_specs=_tile_specs(seq, ts, True) + [
            mod_spec, _resident((1, D_MODEL)), _resident((D_MODEL, LRU_WIDTH)),
        ] + lru_specs + [state_spec],
        out_specs=pl.BlockSpec((groups, SUBLANES, LRU_WIDTH),
                               lambda b, j: (b * nt + nt - 1 - j, 0, 0)),
        out_shape=jax.ShapeDtypeStruct((n // SUBLANES, SUBLANES, LRU_WIDTH), F32),
        scratch_shapes=scan_scratch + [carry_scratch],
        compiler_params=_cparams(2),
        name="mix_bwd",
    )(x, x, x, mod, g2, w_in[:, :LRU_WIDTH], *lru_args, h0_b)
    hb = hb.reshape(n, LRU_WIDTH)

    out, h_last = pl.pallas_call(
        functools.partial(_mix_fwd_kernel, ts=ts, nt=nt),
        grid=(bsz, nt),
        in_specs=_tile_specs(seq, ts, False) + [
            pl.BlockSpec((ts, LRU_WIDTH), lambda b, j: (b * nt + j, 0)),
            mod_spec, _resident((1, D_MODEL)), _resident((D_MODEL, IN_PROJ_WIDTH)),
        ] + lru_specs + [
            state_spec,
            _resident((1, MLP_WIDTH)),
            _resident((MLP_GROUPS, CHUNK, CHUNK)),
            _resident((CHUNK, MLP_WIDTH)),
            _resident((D_MODEL, D_MODEL)),
        ],
        out_specs=[
            pl.BlockSpec((ts, D_MODEL), lambda b, j: (b * nt + j, 0)),
            pl.BlockSpec((1, SUBLANES, LRU_WIDTH), lambda b, j: (b, 0, 0)),
        ],
        out_shape=[
            jax.ShapeDtypeStruct((n, D_MODEL), F32),
            jax.ShapeDtypeStruct((bsz, SUBLANES, LRU_WIDTH), F32),
        ],
        scratch_shapes=scan_scratch + [
            pltpu.VMEM((groups, SUBLANES, LRU_WIDTH), F32),
            pltpu.VMEM((ts, LRU_WIDTH), F32),
            pltpu.VMEM((ts, MLP_WIDTH), F32),
            pltpu.VMEM((ts, MLP_WIDTH), BF16),
            pltpu.VMEM((ts, D_MODEL), BF16),
            carry_scratch,
        ],
        compiler_params=_cparams(2),
        name="mix_fwd",
    )(x, x, x, hb, mod, g2, w_in, *lru_args, h0_f,
      sgu_g.reshape(1, MLP_WIDTH), sgu_w, sgu_bias, w_out)
    return out, h_last[:, 0:1, :], hb


def _gate_weights(w_r, w_i):
    hpb = HALF // LRU_HEAD_DIM
    out = []
    for d in range(2):
        halves = []
        for hh in range(LRU_WIDTH // HALF):
            blocks_r = [w_r[d, hh * hpb + k] for k in range(hpb)]
            blocks_i = [w_i[d, hh * hpb + k] for k in range(hpb)]
            halves.append(jnp.concatenate(
                [jax.scipy.linalg.block_diag(*blocks_r), jax.scipy.linalg.block_diag(*blocks_i)],
                axis=1))
        out.append(jnp.stack(halves))
    return (0.5 * jnp.stack(out)).astype(BF16)


def kernel(x, c, ctx, c_ctx, w_ada, b_ada, ffn1_norm_g, ffn1_w_in, ffn1_w_out,
           mix_norm_g, w_in_mix, lru_conv_w, lru_conv_b, lru_w_r, lru_b_r, lru_w_i,
           lru_b_i, lru_lambda, sgu_norm_g, sgu_w, sgu_b, w_out_mix,
           ffn2_norm_g, ffn2_w_in, ffn2_w_out, final_norm_g):
    bsz, n_lat, d = x.shape
    n_ctx = ctx.shape[1]
    depth = w_ada.shape[0]
    assert bsz + 1 <= MOD_ROWS and d == D_MODEL

    c_rows = jnp.zeros((MOD_ROWS, d), F32).at[:bsz].set(c).at[bsz].set(c_ctx)
    mods = _ada(c_rows, w_ada, b_ada).reshape(depth, MOD_ROWS, N_MOD, d)

    h = x.reshape(bsz * n_lat, d)
    hc = ctx.reshape(bsz * n_ctx, d)
    zeros_state = jnp.zeros((bsz, 1, LRU_WIDTH), F32)

    for l in range(depth):
        last = l == depth - 1
        m = mods[l, :bsz]
        mc = mods[l, bsz:bsz + 1]
        w1_in, w1_out = ffn1_w_in[l].astype(BF16), ffn1_w_out[l].astype(BF16)
        w2_in, w2_out = ffn2_w_in[l].astype(BF16), ffn2_w_out[l].astype(BF16)
        w_mix_in, w_mix_out = w_in_mix[l].astype(BF16), w_out_mix[l].astype(BF16)
        lru = (lru_conv_w[l], lru_conv_b[l], _gate_weights(lru_w_r[l], lru_w_i[l]),
               lru_b_r[l], lru_b_i[l], lru_lambda[l])
        s_w = sgu_w[l].astype(BF16)
        s_bias = jnp.repeat(sgu_b[l].T, MLP_GROUP_DIM, axis=1)
        sgu = (sgu_norm_g[l], s_w, s_bias, w_mix_out)

        h = _ffn(h, m, 0, ffn1_norm_g[l], w1_in, w1_out)
        hc = _ffn(hc, mc, 0, ffn1_norm_g[l], w1_in, w1_out, tm=256)

        hc, hf_last, hb_c = _mix(hc, n_ctx, mc, mix_norm_g[l], w_mix_in, lru,
                                 zeros_state, zeros_state, *sgu, ts=n_ctx)
        h0_b = hb_c.reshape(bsz, n_ctx, LRU_WIDTH)[:, 0:1, :]
        h, _, _ = _mix(h, n_lat, m, mix_norm_g[l], w_mix_in, lru, hf_last, h0_b, *sgu, ts=512)

        h = _ffn(h, m, 6, ffn2_norm_g[l], w2_in, w2_out, final_norm_g if last else None)
        if not last:
            hc = _ffn(hc, mc, 6, ffn2_norm_g[l], w2_in, w2_out, tm=256)

    return h.reshape(bsz, n_lat, d)
```

```python
import functools

import jax
import jax.numpy as jnp
from jax import lax
from jax.experimental import pallas as pl
from jax.experimental.pallas import tpu as pltpu

F32 = jnp.float32
BF16 = jnp.bfloat16

EPS = 1e-6
D_MODEL = 1024
D_FF = 2816
LRU_WIDTH = 512
LRU_HEADS = 8
LRU_HEAD_DIM = 64
RG_C = 8.0
MLP_GROUPS = 4
MLP_GROUP_DIM = 128
MLP_WIDTH = 512
CHUNK = 128
IN_PROJ_WIDTH = 2048
N_MOD = 9
MOD_ROWS = 8
SUBLANES = 8
HALF = 256
N_HALVES = LRU_WIDTH // HALF
HALO = 16
TINY = 1e-30
FFN_CHUNK = 512

VMEM_LIMIT = 56 * 1024 * 1024


def _cparams(n_axes):
    return pltpu.CompilerParams(
        dimension_semantics=("arbitrary",) * n_axes,
        vmem_limit_bytes=VMEM_LIMIT,
    )


def _resident(shape):
    nd = len(shape)
    return pl.BlockSpec(shape, lambda *_: (0,) * nd, pipeline_mode=pl.Buffered(1))


def _sigmoid(x):
    return 1.0 / (1.0 + jnp.exp(-x))


def _softplus(x):
    return jnp.maximum(x, 0.0) + jnp.log1p(jnp.exp(-jnp.abs(x)))


def _gelu_tanh(x):
    return 0.5 * x * (1.0 + jnp.tanh(0.7978845608028654 * (x + 0.044715 * (x * x * x))))


def _norm_mod(x, g, shift, scale):
    ms = jnp.mean(x * x, axis=-1, keepdims=True)
    return (x * lax.rsqrt(ms + EPS)) * (g * (1.0 + scale)) + shift


def _ada_kernel(c_ref, w_ref, b_ref, o_ref):
    c = c_ref[...]
    sc = c * _sigmoid(c)
    o_ref[0] = jnp.dot(sc, w_ref[0], preferred_element_type=F32,
                       precision=lax.Precision.HIGHEST) + b_ref[0]


def _ada(c_rows, w_ada, b_ada):
    depth = w_ada.shape[0]
    return pl.pallas_call(
        _ada_kernel,
        grid=(depth, N_MOD),
        in_specs=[
            pl.BlockSpec((MOD_ROWS, D_MODEL), lambda l, j: (0, 0)),
            pl.BlockSpec((1, D_MODEL, D_MODEL), lambda l, j: (l, 0, j)),
            pl.BlockSpec((1, 1, D_MODEL), lambda l, j: (l, 0, j)),
        ],
        out_specs=pl.BlockSpec((1, MOD_ROWS, D_MODEL), lambda l, j: (l, 0, j)),
        out_shape=jax.ShapeDtypeStruct((depth, MOD_ROWS, N_MOD * D_MODEL), F32),
        compiler_params=_cparams(2),
        name="ada_mod",
    )(c_rows, w_ada, b_ada.reshape(depth, 1, N_MOD * D_MODEL))


def _ffn_kernel(*refs, k0, pro, epi):
    refs = list(refs)
    x_ref, mod_ref, g_ref, win_ref, wout_ref = refs[:5]
    pos = 5
    if pro == "outproj":
        y_ref, wo_ref = refs[pos:pos + 2]
        pos += 2
    if epi == "inproj":
        gm_ref, wmix_ref, sg_ref = refs[pos:pos + 3]
        pos += 3
    elif epi == "final":
        fg_ref = refs[pos]
        pos += 1
    o_ref = refs[pos]
    pos += 1
    if epi == "inproj":
        xl_ref, gg_ref, u_ref, vn_ref = refs[pos:pos + 4]
        pos += 4
    act_ref = refs[pos]

    h = x_ref[...]
    if pro == "outproj":
        h = h + mod_ref[0, 5:6, :] * jnp.dot(y_ref[...], wo_ref[...], preferred_element_type=F32)
    shift = mod_ref[0, k0:k0 + 1, :]
    scale = mod_ref[0, k0 + 1:k0 + 2, :]
    gate = mod_ref[0, k0 + 2:k0 + 3, :]
    zb = _norm_mod(h, g_ref[...], shift, scale).astype(BF16)
    for c0 in range(0, D_FF, FFN_CHUNK):
        cw = min(FFN_CHUNK, D_FF - c0)
        g = jnp.dot(zb, win_ref[:, c0:c0 + cw], preferred_element_type=F32)
        u = jnp.dot(zb, win_ref[:, D_FF + c0:D_FF + c0 + cw], preferred_element_type=F32)
        act_ref[:, c0:c0 + cw] = (g * _sigmoid(g) * u).astype(BF16)
    out = h + (0.5 * gate) * jnp.dot(act_ref[...], wout_ref[...], preferred_element_type=F32)
    if epi == "final":
        ms = jnp.mean(out * out, axis=-1, keepdims=True)
        out = out * lax.rsqrt(ms + EPS) * fg_ref[...]
    o_ref[...] = out
    if epi == "inproj":
        zm = _norm_mod(out, gm_ref[...], mod_ref[0, 3:4, :], mod_ref[0, 4:5, :]).astype(BF16)
        xl_ref[...] = jnp.dot(zm, wmix_ref[:, 0:LRU_WIDTH], preferred_element_type=F32)
        gl = jnp.dot(zm, wmix_ref[:, LRU_WIDTH:2 * LRU_WIDTH], preferred_element_type=F32)
        gg_ref[...] = _gelu_tanh(gl)
        u_ref[...] = jnp.dot(zm, wmix_ref[:, 2 * LRU_WIDTH:2 * LRU_WIDTH + MLP_WIDTH],
                             preferred_element_type=F32)
        v = jnp.dot(zm, wmix_ref[:, 2 * LRU_WIDTH + MLP_WIDTH:], preferred_element_type=F32)
        vc = v - jnp.mean(v, axis=-1, keepdims=True)
        var = jnp.mean(vc * vc, axis=-1, keepdims=True)
        vn_ref[...] = (vc * lax.rsqrt(var + EPS) * sg_ref[...]).astype(BF16)


def _ffn(x, mod, k0, norm_g, w_in, w_out, *, tm, outproj=None, inproj=None, final_g=None):
    n = x.shape[0]
    tiles_per_mod = n // mod.shape[0] // tm
    pro = "outproj" if outproj is not None else None
    epi = "inproj" if inproj is not None else ("final" if final_g is not None else None)

    def row_spec(width):
        return pl.BlockSpec((tm, width), lambda i: (i, 0))

    args = [x, mod, norm_g.reshape(1, D_MODEL), w_in, w_out]
    in_specs = [
        row_spec(D_MODEL),
        pl.BlockSpec((1, N_MOD, D_MODEL), lambda i: (i // tiles_per_mod, 0, 0)),
        _resident((1, D_MODEL)),
        _resident((D_MODEL, 2 * D_FF)),
        _resident((D_FF, D_MODEL)),
    ]
    out_shape = [jax.ShapeDtypeStruct((n, D_MODEL), F32)]
    out_specs = [row_spec(D_MODEL)]
    if pro == "outproj":
        args += [outproj[0], outproj[1]]
        in_specs += [row_spec(D_MODEL), _resident((D_MODEL, D_MODEL))]
    if epi == "inproj":
        args += [inproj[0].reshape(1, D_MODEL), inproj[1], inproj[2].reshape(1, MLP_WIDTH)]
        in_specs += [_resident((1, D_MODEL)), _resident((D_MODEL, IN_PROJ_WIDTH)),
                     _resident((1, MLP_WIDTH))]
        out_shape += [jax.ShapeDtypeStruct((n, LRU_WIDTH), F32),
                      jax.ShapeDtypeStruct((n, LRU_WIDTH), F32),
                      jax.ShapeDtypeStruct((n, MLP_WIDTH), F32),
                      jax.ShapeDtypeStruct((n, MLP_WIDTH), BF16)]
        out_specs += [row_spec(LRU_WIDTH), row_spec(LRU_WIDTH), row_spec(MLP_WIDTH),
                      row_spec(MLP_WIDTH)]
    elif epi == "final":
        args += [final_g.reshape(1, D_MODEL)]
        in_specs += [_resident((1, D_MODEL))]
    res = pl.pallas_call(
        functools.partial(_ffn_kernel, k0=k0, pro=pro, epi=epi),
        grid=(n // tm,),
        in_specs=in_specs,
        out_specs=out_specs,
        out_shape=out_shape,
        scratch_shapes=[pltpu.VMEM((tm, D_FF), BF16)],
        compiler_params=_cparams(1),
        name="ffn_" + (pro or "plain") + "_" + (epi or "plain"),
    )(*args)
    return res if epi == "inproj" else res[0]


def _load_window(xl_ref, xp_ref, xn_ref, xe_ref, tile, nt, ts):
    xe_ref[0:HALO] = jnp.where(tile > 0, xp_ref[...], 0.0)
    xe_ref[HALO:HALO + ts] = xl_ref[...]
    xe_ref[HALO + ts:] = jnp.where(tile < nt - 1, xn_ref[...], 0.0)


def _short_conv(xe_ref, cw_ref, cb_ref, ts):
    xe = xe_ref[...]
    rows = ts + 2 * HALO
    xc = cb_ref[...] + xe[HALO:HALO + ts] * cw_ref[2:3, :]
    for k, sh in ((0, 2), (1, 1), (3, rows - 1)):
        xc = xc + pltpu.roll(xe, sh, 0)[HALO:HALO + ts] * cw_ref[k:k + 1, :]
    return xc


def _scan_coeffs(d, xc, wg_ref, br_ref, bi_ref, lam_ref, a_ref, b_ref, ts):
    groups = ts // SUBLANES
    sub = lax.broadcasted_iota(jnp.int32, (groups, SUBLANES, HALF), 1)
    for hh in range(N_HALVES):
        cols = slice(hh * HALF, (hh + 1) * HALF)
        xch = xc[:, cols]
        kk = (-0.5 * RG_C) * _softplus(-lam_ref[d:d + 1, cols])
        pre = jnp.dot(xch.astype(BF16), wg_ref[d, hh], preferred_element_type=F32)
        tr = jnp.tanh(pre[:, :HALF] + 0.5 * br_ref[d:d + 1, cols])
        ti = jnp.tanh(pre[:, HALF:] + 0.5 * bi_ref[d:d + 1, cols])
        log_a = kk + kk * tr
        a = jnp.exp(log_a)
        t = jnp.tanh(log_a)
        q = t / (t - 1.0)
        gain = q * lax.rsqrt(jnp.maximum(q, TINY))
        bc = gain * ((0.7071067811865476 * xch) * (1.0 + ti))
        a3 = a.reshape(groups, SUBLANES, HALF)
        b3 = bc.reshape(groups, SUBLANES, HALF)
        for s in (1, 2, 4):
            if d == 0:
                sh, m = s, sub >= s
            else:
                sh, m = SUBLANES - s, sub < SUBLANES - s
            am = jnp.where(m, a3, 0.0)
            b3 = b3 + am * pltpu.roll(b3, sh, 1)
            a3 = jnp.where(m, a3 * pltpu.roll(a3, sh, 1), a3)
        a_ref[:, :, cols] = a3
        b_ref[:, :, cols] = b3


def _carry_scan(d, a_ref, b_ref, h_ref, carry, groups):
    def body(i, c):
        g = i if d == 0 else groups - 1 - i
        h = a_ref[g] * c + b_ref[g]
        h_ref[g] = h
        row = h[SUBLANES - 1:SUBLANES, :] if d == 0 else h[0:1, :]
        return jnp.broadcast_to(row, (SUBLANES, LRU_WIDTH))

    return lax.fori_loop(0, groups, body, carry, unroll=8)


def _lru_bwd_kernel(xl_ref, xp_ref, xn_ref, cw_ref, cb_ref, wg_ref, br_ref, bi_ref, lam_ref,
                    h0_ref, hb_ref, xe_ref, a_ref, b_ref, carry_ref, *, ts, nt):
    j = pl.program_id(1)
    groups = ts // SUBLANES

    @pl.when(j == 0)
    def _():
        carry_ref[...] = jnp.broadcast_to(h0_ref[0], (SUBLANES, LRU_WIDTH))

    _load_window(xl_ref, xp_ref, xn_ref, xe_ref, nt - 1 - j, nt, ts)
    xc = _short_conv(xe_ref, cw_ref, cb_ref, ts)
    _scan_coeffs(1, xc, wg_ref, br_ref, bi_ref, lam_ref, a_ref, b_ref, ts)
    carry_ref[...] = _carry_scan(1, a_ref, b_ref, hb_ref, carry_ref[...], groups)


def _mix_fwd_kernel(xl_ref, xp_ref, xn_ref, gg_ref, u_ref, vn_ref, hb_ref,
                    cw_ref, cb_ref, wg_ref, br_ref, bi_ref, lam_ref, h0_ref, sw_ref, sb_ref,
                    y_ref, hlast_ref, xe_ref, a_ref, b_ref, hf_ref, carry_ref, *, ts, nt):
    j = pl.program_id(1)
    groups = ts // SUBLANES
    n_chunks = ts // CHUNK

    @pl.when(j == 0)
    def _():
        carry_ref[...] = jnp.broadcast_to(h0_ref[0], (SUBLANES, LRU_WIDTH))

    _load_window(xl_ref, xp_ref, xn_ref, xe_ref, j, nt, ts)
    xc = _short_conv(xe_ref, cw_ref, cb_ref, ts)
    _scan_coeffs(0, xc, wg_ref, br_ref, bi_ref, lam_ref, a_ref, b_ref, ts)
    for g in range(MLP_GROUPS):
        cols = slice(g * MLP_GROUP_DIM, (g + 1) * MLP_GROUP_DIM)
        rhs = jnp.concatenate(
            [vn_ref[ch * CHUNK:(ch + 1) * CHUNK, cols] for ch in range(n_chunks)], axis=1)
        z = jnp.dot(sw_ref[g], rhs, preferred_element_type=F32)
        for ch in range(n_chunks):
            rows = slice(ch * CHUNK, (ch + 1) * CHUNK)
            zc = z[:, ch * MLP_GROUP_DIM:(ch + 1) * MLP_GROUP_DIM] + sb_ref[:, cols]
            y_ref[rows, LRU_WIDTH + g * MLP_GROUP_DIM:LRU_WIDTH + (g + 1) * MLP_GROUP_DIM] = (
                u_ref[rows, cols] * zc).astype(BF16)

    carry = _carry_scan(0, a_ref, b_ref, hf_ref, carry_ref[...], groups)
    carry_ref[...] = carry
    hlast_ref[0] = carry

    h_sum = hf_ref[...].reshape(ts, LRU_WIDTH) + hb_ref[...]
    y_ref[:, 0:LRU_WIDTH] = (h_sum * gg_ref[...]).astype(BF16)


def _window_specs(seq, ts, reverse):
    nt = seq // ts
    rb = ts // HALO
    sb = seq // HALO

    def tidx(j):
        return nt - 1 - j if reverse else j

    return [
        pl.BlockSpec((ts, LRU_WIDTH), lambda b, j: (b * nt + tidx(j), 0)),
        pl.BlockSpec((HALO, LRU_WIDTH),
                     lambda b, j: (b * sb + jnp.maximum(tidx(j) * rb - 1, 0), 0)),
        pl.BlockSpec((HALO, LRU_WIDTH),
                     lambda b, j: (b * sb + jnp.minimum((tidx(j) + 1) * rb, sb - 1), 0)),
    ]


def _mix(xl, gg, u, vn, seq, lru, h0_f, h0_b, sgu_w, sgu_bias, *, ts):
    conv_w, conv_b, wg, b_r, b_i, lam = lru
    n = xl.shape[0]
    bsz = n // seq
    nt = seq // ts
    groups = ts // SUBLANES
    state_spec = pl.BlockSpec((1, 1, LRU_WIDTH), lambda b, j: (b, 0, 0))
    lru_specs = [
        _resident((4, LRU_WIDTH)),
        _resident((1, LRU_WIDTH)),
        _resident((2, N_HALVES, HALF, 2 * HALF)),
        _resident((2, LRU_WIDTH)),
        _resident((2, LRU_WIDTH)),
        _resident((2, LRU_WIDTH)),
    ]
    lru_args = (conv_w, conv_b.reshape(1, LRU_WIDTH), wg, b_r, b_i, lam)
    scan_scratch = [
        pltpu.VMEM((ts + 2 * HALO, LRU_WIDTH), F32),
        pltpu.VMEM((groups, SUBLANES, LRU_WIDTH), F32),
        pltpu.VMEM((groups, SUBLANES, LRU_WIDTH), F32),
    ]
    carry_scratch = pltpu.VMEM((SUBLANES, LRU_WIDTH), F32)

    def tile_spec(width):
        return pl.BlockSpec((ts, width), lambda b, j: (b * nt + j, 0))

    hb = pl.pallas_call(
        functools.partial(_lru_bwd_kernel, ts=ts, nt=nt),
        grid=(bsz, nt),
        in_specs=_window_specs(seq, ts, True) + lru_specs + [state_spec],
        out_specs=pl.BlockSpec((groups, SUBLANES, LRU_WIDTH),
                               lambda b, j: (b * nt + nt - 1 - j, 0, 0)),
        out_shape=jax.ShapeDtypeStruct((n // SUBLANES, SUBLANES, LRU_WIDTH), F32),
        scratch_shapes=scan_scratch + [carry_scratch],
        compiler_params=_cparams(2),
        name="lru_bwd",
    )(xl, xl, xl, *lru_args, h0_b)
    hb = hb.reshape(n, LRU_WIDTH)

    y, h_last = pl.pallas_call(
        functools.partial(_mix_fwd_kernel, ts=ts, nt=nt),
        grid=(bsz, nt),
        in_specs=_window_specs(seq, ts, False) + [
            tile_spec(LRU_WIDTH), tile_spec(MLP_WIDTH), tile_spec(MLP_WIDTH), tile_spec(LRU_WIDTH),
        ] + lru_specs + [
            state_spec,
            _resident((MLP_GROUPS, CHUNK, CHUNK)),
            _resident((CHUNK, MLP_WIDTH)),
        ],
        out_specs=[
            tile_spec(D_MODEL),
            pl.BlockSpec((1, SUBLANES, LRU_WIDTH), lambda b, j: (b, 0, 0)),
        ],
        out_shape=[
            jax.ShapeDtypeStruct((n, D_MODEL), BF16),
            jax.ShapeDtypeStruct((bsz, SUBLANES, LRU_WIDTH), F32),
        ],
        scratch_shapes=scan_scratch + [
            pltpu.VMEM((groups, SUBLANES, LRU_WIDTH), F32),
            carry_scratch,
        ],
        compiler_params=_cparams(2),
        name="mix_fwd",
    )(xl, xl, xl, gg, u, vn, hb, *lru_args, h0_f, sgu_w, sgu_bias)
    return y, h_last[:, 0:1, :], hb


def _gate_weights(w_r, w_i):
    hpb = HALF // LRU_HEAD_DIM
    out = []
    for d in range(2):
        halves = []
        for hh in range(N_HALVES):
            blocks_r = [w_r[d, hh * hpb + k] for k in range(hpb)]
            blocks_i = [w_i[d, hh * hpb + k] for k in range(hpb)]
            halves.append(jnp.concatenate(
                [jax.scipy.linalg.block_diag(*blocks_r), jax.scipy.linalg.block_diag(*blocks_i)],
                axis=1))
        out.append(jnp.stack(halves))
    return (0.5 * jnp.stack(out)).astype(BF16)


def kernel(x, c, ctx, c_ctx, w_ada, b_ada, ffn1_norm_g, ffn1_w_in, ffn1_w_out,
           mix_norm_g, w_in_mix, lru_conv_w, lru_conv_b, lru_w_r, lru_b_r, lru_w_i,
           lru_b_i, lru_lambda, sgu_norm_g, sgu_w, sgu_b, w_out_mix,
           ffn2_norm_g, ffn2_w_in, ffn2_w_out, final_norm_g):
    bsz, n_lat, d = x.shape
    n_ctx = ctx.shape[1]
    depth = w_ada.shape[0]
    assert bsz + 1 <= MOD_ROWS and d == D_MODEL
    tm_lat, tm_ctx = 512, n_ctx

    c_rows = jnp.zeros((MOD_ROWS, d), F32).at[:bsz].set(c).at[bsz].set(c_ctx)
    mods = _ada(c_rows, w_ada, b_ada).reshape(depth, MOD_ROWS, N_MOD, d)

    h = x.reshape(bsz * n_lat, d)
    hc = ctx.reshape(bsz * n_ctx, d)
    zeros_state = jnp.zeros((bsz, 1, LRU_WIDTH), F32)

    for l in range(depth):
        last = l == depth - 1
        m = mods[l, :bsz]
        mc = mods[l, bsz:bsz + 1]
        w1_in, w1_out = ffn1_w_in[l].astype(BF16), ffn1_w_out[l].astype(BF16)
        w2_in, w2_out = ffn2_w_in[l].astype(BF16), ffn2_w_out[l].astype(BF16)
        w_mix_in, w_mix_out = w_in_mix[l].astype(BF16), w_out_mix[l].astype(BF16)
        lru = (lru_conv_w[l], lru_conv_b[l], _gate_weights(lru_w_r[l], lru_w_i[l]),
               lru_b_r[l], lru_b_i[l], lru_lambda[l])
        s_w = sgu_w[l].astype(BF16)
        s_bias = jnp.repeat(sgu_b[l].T, MLP_GROUP_DIM, axis=1)
        inproj = (mix_norm_g[l], w_mix_in, sgu_norm_g[l])

        hc, xl, gg, u, vn = _ffn(hc, mc, 0, ffn1_norm_g[l], w1_in, w1_out, tm=tm_ctx,
                                 inproj=inproj)
        yc, hf_last, hb_c = _mix(xl, gg, u, vn, n_ctx, lru, zeros_state, zeros_state,
                                 s_w, s_bias, ts=n_ctx)
        h0_b = hb_c.reshape(bsz, n_ctx, LRU_WIDTH)[:, 0:1, :]
        if not last:
            hc = _ffn(hc, mc, 6, ffn2_norm_g[l], w2_in, w2_out, tm=tm_ctx,
                      outproj=(yc, w_mix_out))

        h, xl, gg, u, vn = _ffn(h, m, 0, ffn1_norm_g[l], w1_in, w1_out, tm=tm_lat,
                                inproj=inproj)
        y, _, _ = _mix(xl, gg, u, vn, n_lat, lru, hf_last, h0_b, s_w, s_bias, ts=512)
        h = _ffn(h, m, 6, ffn2_norm_g[l], w2_in, w2_out, tm=tm_lat, outproj=(y, w_mix_out),
                 final_g=final_norm_g if last else None)

    return h.reshape(bsz, n_lat, d)
```

```python
import functools

import jax
import jax.numpy as jnp
from jax import lax
from jax.experimental import pallas as pl
from jax.experimental.pallas import tpu as pltpu

F32 = jnp.float32
BF16 = jnp.bfloat16

EPS = 1e-6
D_MODEL = 1024
D_FF = 2816
LRU_WIDTH = 512
LRU_HEADS = 8
LRU_HEAD_DIM = 64
RG_C = 8.0
MLP_GROUPS = 4
MLP_GROUP_DIM = 128
MLP_WIDTH = 512
CHUNK = 128
IN_PROJ_WIDTH = 2048
N_MOD = 9
MOD_ROWS = 8
SUBLANES = 8
HALF = 256
N_HALVES = LRU_WIDTH // HALF
TINY = 1e-30
FFN_CHUNK = 512

VMEM_LIMIT = 56 * 1024 * 1024


def _cparams(n_axes):
    return pltpu.CompilerParams(
        dimension_semantics=("arbitrary",) * n_axes,
        vmem_limit_bytes=VMEM_LIMIT,
    )


def _resident(shape):
    nd = len(shape)
    return pl.BlockSpec(shape, lambda *_: (0,) * nd, pipeline_mode=pl.Buffered(1))


def _sigmoid(x):
    return 1.0 / (1.0 + jnp.exp(-x))


def _softplus(x):
    return jnp.maximum(x, 0.0) + jnp.log1p(jnp.exp(-jnp.abs(x)))


def _gelu_tanh(x):
    return 0.5 * x * (1.0 + jnp.tanh(0.7978845608028654 * (x + 0.044715 * (x * x * x))))


def _norm_mod(x, g, shift, scale):
    ms = jnp.mean(x * x, axis=-1, keepdims=True)
    return (x * lax.rsqrt(ms + EPS)) * (g * (1.0 + scale)) + shift


def _ada_kernel(c_ref, w_ref, b_ref, o_ref):
    c = c_ref[...]
    sc = c * _sigmoid(c)
    o_ref[0] = jnp.dot(sc, w_ref[0], preferred_element_type=F32,
                       precision=lax.Precision.HIGHEST) + b_ref[0]


def _ada(c_rows, w_ada, b_ada):
    depth = w_ada.shape[0]
    return pl.pallas_call(
        _ada_kernel,
        grid=(depth, N_MOD),
        in_specs=[
            pl.BlockSpec((MOD_ROWS, D_MODEL), lambda l, j: (0, 0)),
            pl.BlockSpec((1, D_MODEL, D_MODEL), lambda l, j: (l, 0, j)),
            pl.BlockSpec((1, 1, D_MODEL), lambda l, j: (l, 0, j)),
        ],
        out_specs=pl.BlockSpec((1, MOD_ROWS, D_MODEL), lambda l, j: (l, 0, j)),
        out_shape=jax.ShapeDtypeStruct((depth, MOD_ROWS, N_MOD * D_MODEL), F32),
        compiler_params=_cparams(2),
        name="ada_mod",
    )(c_rows, w_ada, b_ada.reshape(depth, 1, N_MOD * D_MODEL))


def _ffn_kernel(*refs, k0, pro, epi):
    refs = list(refs)
    x_ref, mod_ref, g_ref, win_ref, wout_ref = refs[:5]
    pos = 5
    if pro == "outproj":
        y_ref, wo_ref = refs[pos:pos + 2]
        pos += 2
    if epi == "inproj":
        gm_ref, wmix_ref, sg_ref, cw_ref, cb_ref = refs[pos:pos + 5]
        pos += 5
    elif epi == "final":
        fg_ref = refs[pos]
        pos += 1
    o_ref = refs[pos]
    pos += 1
    if epi == "inproj":
        xl_ref, xc_ref, gg_ref, u_ref, vn_ref = refs[pos:pos + 5]
        pos += 5
    act_ref = refs[pos]
    pos += 1
    if epi == "inproj":
        xe_ref = refs[pos]
        tm = x_ref.shape[0]

        @pl.when(pl.program_id(0) == 0)
        def _():
            xe_ref[...] = jnp.zeros_like(xe_ref)

    h = x_ref[...]
    if pro == "outproj":
        h = h + mod_ref[0, 5:6, :] * jnp.dot(y_ref[...], wo_ref[...], preferred_element_type=F32)
    shift = mod_ref[0, k0:k0 + 1, :]
    scale = mod_ref[0, k0 + 1:k0 + 2, :]
    gate = mod_ref[0, k0 + 2:k0 + 3, :]
    zb = _norm_mod(h, g_ref[...], shift, scale).astype(BF16)
    for c0 in range(0, D_FF, FFN_CHUNK):
        cw = min(FFN_CHUNK, D_FF - c0)
        g = jnp.dot(zb, win_ref[:, c0:c0 + cw], preferred_element_type=F32)
        u = jnp.dot(zb, win_ref[:, D_FF + c0:D_FF + c0 + cw], preferred_element_type=F32)
        act_ref[:, c0:c0 + cw] = (g * _sigmoid(g) * u).astype(BF16)
    out = h + (0.5 * gate) * jnp.dot(act_ref[...], wout_ref[...], preferred_element_type=F32)
    if epi == "final":
        ms = jnp.mean(out * out, axis=-1, keepdims=True)
        out = out * lax.rsqrt(ms + EPS) * fg_ref[...]
    o_ref[...] = out
    if epi == "inproj":
        zm = _norm_mod(out, gm_ref[...], mod_ref[0, 3:4, :], mod_ref[0, 4:5, :]).astype(BF16)
        xl = jnp.dot(zm, wmix_ref[:, 0:LRU_WIDTH], preferred_element_type=F32)
        xl_ref[...] = xl
        xe_ref[SUBLANES:SUBLANES + tm] = xl
        xc_ref[...] = _short_conv(xe_ref, cw_ref, cb_ref, tm, SUBLANES)
        gl = jnp.dot(zm, wmix_ref[:, LRU_WIDTH:2 * LRU_WIDTH], preferred_element_type=F32)
        gg_ref[...] = _gelu_tanh(gl)
        u_ref[...] = jnp.dot(zm, wmix_ref[:, 2 * LRU_WIDTH:2 * LRU_WIDTH + MLP_WIDTH],
                             preferred_element_type=F32)
        v = jnp.dot(zm, wmix_ref[:, 2 * LRU_WIDTH + MLP_WIDTH:], preferred_element_type=F32)
        vc = v - jnp.mean(v, axis=-1, keepdims=True)
        var = jnp.mean(vc * vc, axis=-1, keepdims=True)
        vn_ref[...] = (vc * lax.rsqrt(var + EPS) * sg_ref[...]).astype(BF16)


def _ffn(x, mod, k0, norm_g, w_in, w_out, *, tm, outproj=None, inproj=None, final_g=None):
    n = x.shape[0]
    tiles_per_mod = n // mod.shape[0] // tm
    pro = "outproj" if outproj is not None else None
    epi = "inproj" if inproj is not None else ("final" if final_g is not None else None)

    def row_spec(width):
        return pl.BlockSpec((tm, width), lambda i: (i, 0))

    args = [x, mod, norm_g.reshape(1, D_MODEL), w_in, w_out]
    in_specs = [
        row_spec(D_MODEL),
        pl.BlockSpec((1, N_MOD, D_MODEL), lambda i: (i // tiles_per_mod, 0, 0)),
        _resident((1, D_MODEL)),
        _resident((D_MODEL, 2 * D_FF)),
        _resident((D_FF, D_MODEL)),
    ]
    out_shape = [jax.ShapeDtypeStruct((n, D_MODEL), F32)]
    out_specs = [row_spec(D_MODEL)]
    if pro == "outproj":
        args += [outproj[0], outproj[1]]
        in_specs += [row_spec(D_MODEL), _resident((D_MODEL, D_MODEL))]
    scratch = [pltpu.VMEM((tm, D_FF), BF16)]
    if epi == "inproj":
        args += [inproj[0].reshape(1, D_MODEL), inproj[1], inproj[2].reshape(1, MLP_WIDTH),
                 inproj[3], inproj[4].reshape(1, LRU_WIDTH)]
        in_specs += [_resident((1, D_MODEL)), _resident((D_MODEL, IN_PROJ_WIDTH)),
                     _resident((1, MLP_WIDTH)), _resident((4, LRU_WIDTH)),
                     _resident((1, LRU_WIDTH))]
        out_shape += [jax.ShapeDtypeStruct((n, LRU_WIDTH), F32),
                      jax.ShapeDtypeStruct((n, LRU_WIDTH), F32),
                      jax.ShapeDtypeStruct((n, LRU_WIDTH), F32),
                      jax.ShapeDtypeStruct((n, MLP_WIDTH), F32),
                      jax.ShapeDtypeStruct((n, MLP_WIDTH), BF16)]
        out_specs += [row_spec(LRU_WIDTH), row_spec(LRU_WIDTH), row_spec(LRU_WIDTH),
                      row_spec(MLP_WIDTH), row_spec(MLP_WIDTH)]
        scratch += [pltpu.VMEM((tm + 2 * SUBLANES, LRU_WIDTH), F32)]
    elif epi == "final":
        args += [final_g.reshape(1, D_MODEL)]
        in_specs += [_resident((1, D_MODEL))]
    res = pl.pallas_call(
        functools.partial(_ffn_kernel, k0=k0, pro=pro, epi=epi),
        grid=(n // tm,),
        in_specs=in_specs,
        out_specs=out_specs,
        out_shape=out_shape,
        scratch_shapes=scratch,
        compiler_params=_cparams(1),
        name="ffn_" + (pro or "plain") + "_" + (epi or "plain"),
    )(*args)
    return res if epi == "inproj" else res[0]


def _short_conv(xe_ref, cw_ref, cb_ref, ts, halo):
    xe = xe_ref[...]
    rows = ts + 2 * halo
    xc = cb_ref[...] + xe[halo:halo + ts] * cw_ref[2:3, :]
    for k, sh in ((0, 2), (1, 1), (3, rows - 1)):
        xc = xc + pltpu.roll(xe, sh, 0)[halo:halo + ts] * cw_ref[k:k + 1, :]
    return xc


def _finish_conv(xc_ref, xp_ref, xn_ref, cw_ref, xcs_ref, tile, nt, ts):
    zeros = jnp.zeros((SUBLANES, LRU_WIDTH), F32)
    e = jnp.concatenate([jnp.where(tile > 0, xp_ref[...], 0.0), zeros], axis=0)
    head = (pltpu.roll(e, 2, 0)[SUBLANES:] * cw_ref[0:1, :]
            + pltpu.roll(e, 1, 0)[SUBLANES:] * cw_ref[1:2, :])
    f = jnp.concatenate([zeros, jnp.where(tile < nt - 1, xn_ref[...], 0.0)], axis=0)
    tail = pltpu.roll(f, 2 * SUBLANES - 1, 0)[0:SUBLANES] * cw_ref[3:4, :]
    xcs_ref[...] = xc_ref[...]
    xcs_ref[0:SUBLANES] = xcs_ref[0:SUBLANES] + head
    xcs_ref[ts - SUBLANES:ts] = xcs_ref[ts - SUBLANES:ts] + tail
    return xcs_ref[...]


def _scan_coeffs(d, xc, wg_ref, br_ref, bi_ref, lam_ref, a_ref, b_ref, ts):
    groups = ts // SUBLANES
    sub = lax.broadcasted_iota(jnp.int32, (groups, SUBLANES, HALF), 1)
    for hh in range(N_HALVES):
        cols = slice(hh * HALF, (hh + 1) * HALF)
        xch = xc[:, cols]
        kk = (-0.5 * RG_C) * _softplus(-lam_ref[d:d + 1, cols])
        pre = jnp.dot(xch.astype(BF16), wg_ref[d, hh], preferred_element_type=F32)
        tr = jnp.tanh(pre[:, :HALF] + 0.5 * br_ref[d:d + 1, cols])
        ti = jnp.tanh(pre[:, HALF:] + 0.5 * bi_ref[d:d + 1, cols])
        log_a = kk + kk * tr
        a = jnp.exp(log_a)
        t = jnp.tanh(log_a)
        q = t / (t - 1.0)
        gain = q * lax.rsqrt(jnp.maximum(q, TINY))
        bc = gain * ((0.7071067811865476 * xch) * (1.0 + ti))
        a3 = a.reshape(groups, SUBLANES, HALF)
        b3 = bc.reshape(groups, SUBLANES, HALF)
        for s in (1, 2, 4):
            if d == 0:
                sh, m = s, sub >= s
            else:
                sh, m = SUBLANES - s, sub < SUBLANES - s
            am = jnp.where(m, a3, 0.0)
            b3 = b3 + am * pltpu.roll(b3, sh, 1)
            a3 = jnp.where(m, a3 * pltpu.roll(a3, sh, 1), a3)
        a_ref[:, :, cols] = a3
        b_ref[:, :, cols] = b3


def _carry_scan(d, a_ref, b_ref, h_ref, carry, groups):
    def body(i, c):
        g = i if d == 0 else groups - 1 - i
        h = a_ref[g] * c + b_ref[g]
        h_ref[g] = h
        row = h[SUBLANES - 1:SUBLANES, :] if d == 0 else h[0:1, :]
        return jnp.broadcast_to(row, (SUBLANES, LRU_WIDTH))

    return lax.fori_loop(0, groups, body, carry, unroll=8)


def _lru_bwd_kernel(xc_ref, xp_ref, xn_ref, cw_ref, wg_ref, br_ref, bi_ref, lam_ref,
                    h0_ref, hb_ref, xcs_ref, a_ref, b_ref, carry_ref, *, ts, nt):
    j = pl.program_id(1)
    groups = ts // SUBLANES

    @pl.when(j == 0)
    def _():
        carry_ref[...] = jnp.broadcast_to(h0_ref[0], (SUBLANES, LRU_WIDTH))

    xc = _finish_conv(xc_ref, xp_ref, xn_ref, cw_ref, xcs_ref, nt - 1 - j, nt, ts)
    _scan_coeffs(1, xc, wg_ref, br_ref, bi_ref, lam_ref, a_ref, b_ref, ts)
    carry_ref[...] = _carry_scan(1, a_ref, b_ref, hb_ref, carry_ref[...], groups)


def _mix_fwd_kernel(xc_ref, xp_ref, xn_ref, gg_ref, u_ref, vn_ref, hb_ref,
                    cw_ref, wg_ref, br_ref, bi_ref, lam_ref, h0_ref, sw_ref, sb_ref,
                    y_ref, hlast_ref, xcs_ref, a_ref, b_ref, hf_ref, carry_ref, *, ts, nt):
    j = pl.program_id(1)
    groups = ts // SUBLANES
    n_chunks = ts // CHUNK

    @pl.when(j == 0)
    def _():
        carry_ref[...] = jnp.broadcast_to(h0_ref[0], (SUBLANES, LRU_WIDTH))

    xc = _finish_conv(xc_ref, xp_ref, xn_ref, cw_ref, xcs_ref, j, nt, ts)
    _scan_coeffs(0, xc, wg_ref, br_ref, bi_ref, lam_ref, a_ref, b_ref, ts)
    for g in range(MLP_GROUPS):
        cols = slice(g * MLP_GROUP_DIM, (g + 1) * MLP_GROUP_DIM)
        rhs = jnp.concatenate(
            [vn_ref[ch * CHUNK:(ch + 1) * CHUNK, cols] for ch in range(n_chunks)], axis=1)
        z = jnp.dot(sw_ref[g], rhs, preferred_element_type=F32)
        for ch in range(n_chunks):
            rows = slice(ch * CHUNK, (ch + 1) * CHUNK)
            zc = z[:, ch * MLP_GROUP_DIM:(ch + 1) * MLP_GROUP_DIM] + sb_ref[:, cols]
            y_ref[rows, LRU_WIDTH + g * MLP_GROUP_DIM:LRU_WIDTH + (g + 1) * MLP_GROUP_DIM] = (
                u_ref[rows, cols] * zc).astype(BF16)

    carry = _carry_scan(0, a_ref, b_ref, hf_ref, carry_ref[...], groups)
    carry_ref[...] = carry
    hlast_ref[0] = carry

    h_sum = hf_ref[...].reshape(ts, LRU_WIDTH) + hb_ref[...]
    y_ref[:, 0:LRU_WIDTH] = (h_sum * gg_ref[...]).astype(BF16)


def _window_specs(seq, ts, reverse):
    nt = seq // ts
    rb = ts // SUBLANES
    sb = seq // SUBLANES

    def tidx(j):
        return nt - 1 - j if reverse else j

    return [
        pl.BlockSpec((ts, LRU_WIDTH), lambda b, j: (b * nt + tidx(j), 0)),
        pl.BlockSpec((SUBLANES, LRU_WIDTH),
                     lambda b, j: (b * sb + jnp.maximum(tidx(j) * rb - 1, 0), 0)),
        pl.BlockSpec((SUBLANES, LRU_WIDTH),
                     lambda b, j: (b * sb + jnp.minimum((tidx(j) + 1) * rb, sb - 1), 0)),
    ]


def _mix(xl, xc, gg, u, vn, seq, lru, h0_f, h0_b, sgu_w, sgu_bias, *, ts):
    conv_w, wg, b_r, b_i, lam = lru
    n = xl.shape[0]
    bsz = n // seq
    nt = seq // ts
    groups = ts // SUBLANES
    state_spec = pl.BlockSpec((1, 1, LRU_WIDTH), lambda b, j: (b, 0, 0))
    lru_specs = [
        _resident((4, LRU_WIDTH)),
        _resident((2, N_HALVES, HALF, 2 * HALF)),
        _resident((2, LRU_WIDTH)),
        _resident((2, LRU_WIDTH)),
        _resident((2, LRU_WIDTH)),
    ]
    lru_args = (conv_w, wg, b_r, b_i, lam)
    scan_scratch = [
        pltpu.VMEM((ts, LRU_WIDTH), F32),
        pltpu.VMEM((groups, SUBLANES, LRU_WIDTH), F32),
        pltpu.VMEM((groups, SUBLANES, LRU_WIDTH), F32),
    ]
    carry_scratch = pltpu.VMEM((SUBLANES, LRU_WIDTH), F32)

    def tile_spec(width):
        return pl.BlockSpec((ts, width), lambda b, j: (b * nt + j, 0))

    hb = pl.pallas_call(
        functools.partial(_lru_bwd_kernel, ts=ts, nt=nt),
        grid=(bsz, nt),
        in_specs=_window_specs(seq, ts, True) + lru_specs + [state_spec],
        out_specs=pl.BlockSpec((groups, SUBLANES, LRU_WIDTH),
                               lambda b, j: (b * nt + nt - 1 - j, 0, 0)),
        out_shape=jax.ShapeDtypeStruct((n // SUBLANES, SUBLANES, LRU_WIDTH), F32),
        scratch_shapes=scan_scratch + [carry_scratch],
        compiler_params=_cparams(2),
        name="lru_bwd",
    )(xc, xl, xl, *lru_args, h0_b)
    hb = hb.reshape(n, LRU_WIDTH)

    y, h_last = pl.pallas_call(
        functools.partial(_mix_fwd_kernel, ts=ts, nt=nt),
        grid=(bsz, nt),
        in_specs=_window_specs(seq, ts, False) + [
            tile_spec(LRU_WIDTH), tile_spec(MLP_WIDTH), tile_spec(MLP_WIDTH), tile_spec(LRU_WIDTH),
        ] + lru_specs + [
            state_spec,
            _resident((MLP_GROUPS, CHUNK, CHUNK)),
            _resident((CHUNK, MLP_WIDTH)),
        ],
        out_specs=[
            tile_spec(D_MODEL),
            pl.BlockSpec((1, SUBLANES, LRU_WIDTH), lambda b, j: (b, 0, 0)),
        ],
        out_shape=[
            jax.ShapeDtypeStruct((n, D_MODEL), BF16),
            jax.ShapeDtypeStruct((bsz, SUBLANES, LRU_WIDTH), F32),
        ],
        scratch_shapes=scan_scratch + [
            pltpu.VMEM((groups, SUBLANES, LRU_WIDTH), F32),
            carry_scratch,
        ],
        compiler_params=_cparams(2),
        name="mix_fwd",
    )(xc, xl, xl, gg, u, vn, hb, *lru_args, h0_f, sgu_w, sgu_bias)
    return y, h_last[:, 0:1, :], hb


def _gate_weights(w_r, w_i):
    hpb = HALF // LRU_HEAD_DIM
    out = []
    for d in range(2):
        halves = []
        for hh in range(N_HALVES):
            blocks_r = [w_r[d, hh * hpb + k] for k in range(hpb)]
            blocks_i = [w_i[d, hh * hpb + k] for k in range(hpb)]
            halves.append(jnp.concatenate(
                [jax.scipy.linalg.block_diag(*blocks_r), jax.scipy.linalg.block_diag(*blocks_i)],
                axis=1))
        out.append(jnp.stack(halves))
    return (0.5 * jnp.stack(out)).astype(BF16)


def kernel(x, c, ctx, c_ctx, w_ada, b_ada, ffn1_norm_g, ffn1_w_in, ffn1_w_out,
           mix_norm_g, w_in_mix, lru_conv_w, lru_conv_b, lru_w_r, lru_b_r, lru_w_i,
           lru_b_i, lru_lambda, sgu_norm_g, sgu_w, sgu_b, w_out_mix,
           ffn2_norm_g, ffn2_w_in, ffn2_w_out, final_norm_g):
    bsz, n_lat, d = x.shape
    n_ctx = ctx.shape[1]
    depth = w_ada.shape[0]
    assert bsz + 1 <= MOD_ROWS and d == D_MODEL
    tm_lat, tm_ctx = 512, n_ctx

    c_rows = jnp.zeros((MOD_ROWS, d), F32).at[:bsz].set(c).at[bsz].set(c_ctx)
    mods = _ada(c_rows, w_ada, b_ada).reshape(depth, MOD_ROWS, N_MOD, d)

    h = x.reshape(bsz * n_lat, d)
    hc = ctx.reshape(bsz * n_ctx, d)
    zeros_state = jnp.zeros((bsz, 1, LRU_WIDTH), F32)

    for l in range(depth):
        last = l == depth - 1
        m = mods[l, :bsz]
        mc = mods[l, bsz:bsz + 1]
        w1_in, w1_out = ffn1_w_in[l].astype(BF16), ffn1_w_out[l].astype(BF16)
        w2_in, w2_out = ffn2_w_in[l].astype(BF16), ffn2_w_out[l].astype(BF16)
        w_mix_in, w_mix_out = w_in_mix[l].astype(BF16), w_out_mix[l].astype(BF16)
        lru = (lru_conv_w[l], _gate_weights(lru_w_r[l], lru_w_i[l]),
               lru_b_r[l], lru_b_i[l], lru_lambda[l])
        s_w = sgu_w[l].astype(BF16)
        s_bias = jnp.repeat(sgu_b[l].T, MLP_GROUP_DIM, axis=1)
        inproj = (mix_norm_g[l], w_mix_in, sgu_norm_g[l], lru_conv_w[l], lru_conv_b[l])

        hc, xl, xc, gg, u, vn = _ffn(hc, mc, 0, ffn1_norm_g[l], w1_in, w1_out, tm=tm_ctx,
                                 inproj=inproj)
        yc, hf_last, hb_c = _mix(xl, xc, gg, u, vn, n_ctx, lru, zeros_state, zeros_state,
                                 s_w, s_bias, ts=n_ctx)
        h0_b = hb_c.reshape(bsz, n_ctx, LRU_WIDTH)[:, 0:1, :]
        if not last:
            hc = _ffn(hc, mc, 6, ffn2_norm_g[l], w2_in, w2_out, tm=tm_ctx,
                      outproj=(yc, w_mix_out))

        h, xl, xc, gg, u, vn = _ffn(h, m, 0, ffn1_norm_g[l], w1_in, w1_out, tm=tm_lat,
                                inproj=inproj)
        y, _, _ = _mix(xl, xc, gg, u, vn, n_lat, lru, hf_last, h0_b, s_w, s_bias, ts=tm_lat)
        h = _ffn(h, m, 6, ffn2_norm_g[l], w2_in, w2_out, tm=2 * tm_lat, outproj=(y, w_mix_out),
                 final_g=final_norm_g if last else None)

    return h.reshape(bsz, n_lat, d)
```

```python
import functools

import jax
import jax.numpy as jnp
from jax import lax
from jax.experimental import pallas as pl
from jax.experimental.pallas import tpu as pltpu

F32 = jnp.float32
BF16 = jnp.bfloat16

EPS = 1e-6
D_MODEL = 1024
D_FF = 2816
LRU_WIDTH = 512
LRU_HEADS = 8
LRU_HEAD_DIM = 64
RG_C = 8.0
MLP_GROUPS = 4
MLP_GROUP_DIM = 128
MLP_WIDTH = 512
CHUNK = 128
IN_PROJ_WIDTH = 2048
N_MOD = 9
MOD_ROWS = 8
SUBLANES = 8
HALF = 256
N_HALVES = LRU_WIDTH // HALF
TINY = 1e-30
FFN_CHUNK = 512

VMEM_LIMIT = 56 * 1024 * 1024


def _cparams(n_axes):
    return pltpu.CompilerParams(
        dimension_semantics=("arbitrary",) * n_axes,
        vmem_limit_bytes=VMEM_LIMIT,
    )


def _resident(shape):
    nd = len(shape)
    return pl.BlockSpec(shape, lambda *_: (0,) * nd, pipeline_mode=pl.Buffered(1))


def _sigmoid(x):
    return 1.0 / (1.0 + jnp.exp(-x))


def _softplus(x):
    return jnp.maximum(x, 0.0) + jnp.log1p(jnp.exp(-jnp.abs(x)))


def _gelu_tanh(x):
    return 0.5 * x * (1.0 + jnp.tanh(0.7978845608028654 * (x + 0.044715 * (x * x * x))))


def _norm_mod(x, g, shift, scale):
    ms = jnp.mean(x * x, axis=-1, keepdims=True)
    return (x * lax.rsqrt(ms + EPS)) * (g * (1.0 + scale)) + shift


def _ada_kernel(c_ref, w_ref, b_ref, o_ref):
    c = c_ref[...]
    sc = c * _sigmoid(c)
    w = w_ref[0]
    s_hi = sc.astype(BF16)
    s_lo = (sc - s_hi.astype(F32)).astype(BF16)
    w_hi = w.astype(BF16)
    w_lo = (w - w_hi.astype(F32)).astype(BF16)
    p = jnp.dot(jnp.concatenate([s_hi, s_lo], axis=0), w_hi, preferred_element_type=F32)
    p = p[:MOD_ROWS] + p[MOD_ROWS:] + jnp.dot(s_hi, w_lo, preferred_element_type=F32)
    o_ref[0] = p + b_ref[0]


def _ada(c_rows, w_ada, b_ada):
    depth = w_ada.shape[0]
    return pl.pallas_call(
        _ada_kernel,
        grid=(depth, N_MOD),
        in_specs=[
            pl.BlockSpec((MOD_ROWS, D_MODEL), lambda l, j: (0, 0)),
            pl.BlockSpec((1, D_MODEL, D_MODEL), lambda l, j: (l, 0, j)),
            pl.BlockSpec((1, 1, D_MODEL), lambda l, j: (l, 0, j)),
        ],
        out_specs=pl.BlockSpec((1, MOD_ROWS, D_MODEL), lambda l, j: (l, 0, j)),
        out_shape=jax.ShapeDtypeStruct((depth, MOD_ROWS, N_MOD * D_MODEL), F32),
        compiler_params=_cparams(2),
        name="ada_mod",
    )(c_rows, w_ada, b_ada.reshape(depth, 1, N_MOD * D_MODEL))


def _ffn_kernel(*refs, k0, pro, epi):
    refs = list(refs)
    x_ref, mod_ref, g_ref, win_ref, wout_ref = refs[:5]
    pos = 5
    if pro == "outproj":
        y_ref, wo_ref = refs[pos:pos + 2]
        pos += 2
    if epi == "inproj":
        gm_ref, wmix_ref, sg_ref, cw_ref, cb_ref = refs[pos:pos + 5]
        pos += 5
    elif epi == "final":
        fg_ref = refs[pos]
        pos += 1
    o_ref = refs[pos]
    pos += 1
    if epi == "inproj":
        xl_ref, xc_ref, gg_ref, u_ref, vn_ref = refs[pos:pos + 5]
        pos += 5
    act_ref = refs[pos]
    pos += 1
    if epi == "inproj":
        xe_ref = refs[pos]
        tm = x_ref.shape[0]

        @pl.when(pl.program_id(0) == 0)
        def _():
            xe_ref[...] = jnp.zeros_like(xe_ref)

    h = x_ref[...]
    if pro == "outproj":
        h = h + mod_ref[0, 5:6, :] * jnp.dot(y_ref[...], wo_ref[...], preferred_element_type=F32)
    shift = mod_ref[0, k0:k0 + 1, :]
    scale = mod_ref[0, k0 + 1:k0 + 2, :]
    gate = mod_ref[0, k0 + 2:k0 + 3, :]
    zb = _norm_mod(h, g_ref[...], shift, scale).astype(BF16)
    for c0 in range(0, D_FF, FFN_CHUNK):
        cw = min(FFN_CHUNK, D_FF - c0)
        g = jnp.dot(zb, win_ref[:, c0:c0 + cw], preferred_element_type=F32)
        u = jnp.dot(zb, win_ref[:, D_FF + c0:D_FF + c0 + cw], preferred_element_type=F32)
        act_ref[:, c0:c0 + cw] = (g * _sigmoid(g) * u).astype(BF16)
    out = h + (0.5 * gate) * jnp.dot(act_ref[...], wout_ref[...], preferred_element_type=F32)
    if epi == "final":
        ms = jnp.mean(out * out, axis=-1, keepdims=True)
        out = out * lax.rsqrt(ms + EPS) * fg_ref[...]
    o_ref[...] = out
    if epi == "inproj":
        zm = _norm_mod(out, gm_ref[...], mod_ref[0, 3:4, :], mod_ref[0, 4:5, :]).astype(BF16)
        v = jnp.dot(zm, wmix_ref[:, 2 * LRU_WIDTH + MLP_WIDTH:], preferred_element_type=F32)
        vc = v - jnp.mean(v, axis=-1, keepdims=True)
        var = jnp.mean(vc * vc, axis=-1, keepdims=True)
        vn_ref[...] = (vc * lax.rsqrt(var + EPS) * sg_ref[...]).astype(BF16)
        xl = jnp.dot(zm, wmix_ref[:, 0:LRU_WIDTH], preferred_element_type=F32)
        xl_ref[...] = xl
        xe_ref[SUBLANES:SUBLANES + tm] = xl
        xc_ref[...] = _short_conv(xe_ref, cw_ref, cb_ref, tm, SUBLANES)
        gl = jnp.dot(zm, wmix_ref[:, LRU_WIDTH:2 * LRU_WIDTH], preferred_element_type=F32)
        gg_ref[...] = _gelu_tanh(gl)
        u_ref[...] = jnp.dot(zm, wmix_ref[:, 2 * LRU_WIDTH:2 * LRU_WIDTH + MLP_WIDTH],
                             preferred_element_type=F32)


def _ffn(x, mod, k0, norm_g, w_in, w_out, *, tm, outproj=None, inproj=None, final_g=None):
    n = x.shape[0]
    tiles_per_mod = n // mod.shape[0] // tm
    pro = "outproj" if outproj is not None else None
    epi = "inproj" if inproj is not None else ("final" if final_g is not None else None)

    def row_spec(width):
        return pl.BlockSpec((tm, width), lambda i: (i, 0))

    args = [x, mod, norm_g.reshape(1, D_MODEL), w_in, w_out]
    in_specs = [
        row_spec(D_MODEL),
        pl.BlockSpec((1, N_MOD, D_MODEL), lambda i: (i // tiles_per_mod, 0, 0)),
        _resident((1, D_MODEL)),
        _resident((D_MODEL, 2 * D_FF)),
        _resident((D_FF, D_MODEL)),
    ]
    out_shape = [jax.ShapeDtypeStruct((n, D_MODEL), F32)]
    out_specs = [row_spec(D_MODEL)]
    if pro == "outproj":
        args += [outproj[0], outproj[1]]
        in_specs += [row_spec(D_MODEL), _resident((D_MODEL, D_MODEL))]
    scratch = [pltpu.VMEM((tm, D_FF), BF16)]
    if epi == "inproj":
        args += [inproj[0].reshape(1, D_MODEL), inproj[1], inproj[2].reshape(1, MLP_WIDTH),
                 inproj[3], inproj[4].reshape(1, LRU_WIDTH)]
        in_specs += [_resident((1, D_MODEL)), _resident((D_MODEL, IN_PROJ_WIDTH)),
                     _resident((1, MLP_WIDTH)), _resident((4, LRU_WIDTH)),
                     _resident((1, LRU_WIDTH))]
        out_shape += [jax.ShapeDtypeStruct((n, LRU_WIDTH), F32),
                      jax.ShapeDtypeStruct((n, LRU_WIDTH), F32),
                      jax.ShapeDtypeStruct((n, LRU_WIDTH), F32),
                      jax.ShapeDtypeStruct((n, MLP_WIDTH), F32),
                      jax.ShapeDtypeStruct((n, MLP_WIDTH), BF16)]
        out_specs += [row_spec(LRU_WIDTH), row_spec(LRU_WIDTH), row_spec(LRU_WIDTH),
                      row_spec(MLP_WIDTH), row_spec(MLP_WIDTH)]
        scratch += [pltpu.VMEM((tm + 2 * SUBLANES, LRU_WIDTH), F32)]
    elif epi == "final":
        args += [final_g.reshape(1, D_MODEL)]
        in_specs += [_resident((1, D_MODEL))]
    res = pl.pallas_call(
        functools.partial(_ffn_kernel, k0=k0, pro=pro, epi=epi),
        grid=(n // tm,),
        in_specs=in_specs,
        out_specs=out_specs,
        out_shape=out_shape,
        scratch_shapes=scratch,
        compiler_params=_cparams(1),
        name="ffn_" + (pro or "plain") + "_" + (epi or "plain"),
    )(*args)
    return res if epi == "inproj" else res[0]


def _short_conv(xe_ref, cw_ref, cb_ref, ts, halo):
    xe = xe_ref[...]
    rows = ts + 2 * halo
    xc = cb_ref[...] + xe[halo:halo + ts] * cw_ref[2:3, :]
    for k, sh in ((0, 2), (1, 1), (3, rows - 1)):
        xc = xc + pltpu.roll(xe, sh, 0)[halo:halo + ts] * cw_ref[k:k + 1, :]
    return xc


def _finish_conv(xc_ref, xp_ref, xn_ref, cw_ref, xcs_ref, tile, nt, ts):
    zeros = jnp.zeros((SUBLANES, LRU_WIDTH), F32)
    e = jnp.concatenate([jnp.where(tile > 0, xp_ref[...], 0.0), zeros], axis=0)
    head = (pltpu.roll(e, 2, 0)[SUBLANES:] * cw_ref[0:1, :]
            + pltpu.roll(e, 1, 0)[SUBLANES:] * cw_ref[1:2, :])
    f = jnp.concatenate([zeros, jnp.where(tile < nt - 1, xn_ref[...], 0.0)], axis=0)
    tail = pltpu.roll(f, 2 * SUBLANES - 1, 0)[0:SUBLANES] * cw_ref[3:4, :]
    xcs_ref[...] = xc_ref[...]
    xcs_ref[0:SUBLANES] = xcs_ref[0:SUBLANES] + head
    xcs_ref[ts - SUBLANES:ts] = xcs_ref[ts - SUBLANES:ts] + tail
    return xcs_ref[...]


def _lru_scan(d, xc, wg_ref, br_ref, bi_ref, lam_ref, h_ref, carry, ts):
    groups = ts // SUBLANES
    sub = lax.broadcasted_iota(jnp.int32, (groups, SUBLANES, HALF), 1)
    order = range(groups) if d == 0 else range(groups - 1, -1, -1)
    carry_out = []
    for hh in range(N_HALVES):
        cols = slice(hh * HALF, (hh + 1) * HALF)
        xch = xc[:, cols]
        kk = (-0.5 * RG_C) * _softplus(-lam_ref[d:d + 1, cols])
        pre = jnp.dot(xch.astype(BF16), wg_ref[d, hh], preferred_element_type=F32)
        tr = jnp.tanh(pre[:, :HALF] + 0.5 * br_ref[d:d + 1, cols])
        ti = jnp.tanh(pre[:, HALF:] + 0.5 * bi_ref[d:d + 1, cols])
        log_a = kk + kk * tr
        a = jnp.exp(log_a)
        t = jnp.tanh(log_a)
        q = t / (t - 1.0)
        gain = q * lax.rsqrt(jnp.maximum(q, TINY))
        bc = gain * ((0.7071067811865476 * xch) * (1.0 + ti))
        a3 = a.reshape(groups, SUBLANES, HALF)
        b3 = bc.reshape(groups, SUBLANES, HALF)
        for s in (1, 2, 4):
            if d == 0:
                sh, m = s, sub >= s
            else:
                sh, m = SUBLANES - s, sub < SUBLANES - s
            am = jnp.where(m, a3, 0.0)
            b3 = b3 + am * pltpu.roll(b3, sh, 1)
            a3 = jnp.where(m, a3 * pltpu.roll(a3, sh, 1), a3)
        c = carry[:, cols]
        for g in order:
            h = a3[g] * c + b3[g]
            h_ref[g, :, cols] = h
            row = h[SUBLANES - 1:SUBLANES, :] if d == 0 else h[0:1, :]
            c = jnp.broadcast_to(row, (SUBLANES, HALF))
        carry_out.append(c)
    return jnp.concatenate(carry_out, axis=1)


def _lru_bwd_kernel(xc_ref, xp_ref, xn_ref, cw_ref, wg_ref, br_ref, bi_ref, lam_ref,
                    h0_ref, hb_ref, xcs_ref, carry_ref, *, ts, nt):
    j = pl.program_id(1)

    @pl.when(j == 0)
    def _():
        carry_ref[...] = jnp.broadcast_to(h0_ref[0], (SUBLANES, LRU_WIDTH))

    xc = _finish_conv(xc_ref, xp_ref, xn_ref, cw_ref, xcs_ref, nt - 1 - j, nt, ts)
    carry_ref[...] = _lru_scan(1, xc, wg_ref, br_ref, bi_ref, lam_ref, hb_ref, carry_ref[...], ts)


def _mix_fwd_kernel(xc_ref, xp_ref, xn_ref, gg_ref, u_ref, vn_ref, hb_ref,
                    cw_ref, wg_ref, br_ref, bi_ref, lam_ref, h0_ref, sw_ref, sb_ref,
                    y_ref, hlast_ref, xcs_ref, hf_ref, carry_ref, *, ts, nt):
    j = pl.program_id(1)
    n_chunks = ts // CHUNK

    @pl.when(j == 0)
    def _():
        carry_ref[...] = jnp.broadcast_to(h0_ref[0], (SUBLANES, LRU_WIDTH))

    xc = _finish_conv(xc_ref, xp_ref, xn_ref, cw_ref, xcs_ref, j, nt, ts)
    carry = _lru_scan(0, xc, wg_ref, br_ref, bi_ref, lam_ref, hf_ref, carry_ref[...], ts)
    carry_ref[...] = carry
    hlast_ref[0] = carry
    for g in range(MLP_GROUPS):
        cols = slice(g * MLP_GROUP_DIM, (g + 1) * MLP_GROUP_DIM)
        rhs = jnp.concatenate(
            [vn_ref[ch * CHUNK:(ch + 1) * CHUNK, cols] for ch in range(n_chunks)], axis=1)
        z = jnp.dot(sw_ref[g], rhs, preferred_element_type=F32)
        for ch in range(n_chunks):
            rows = slice(ch * CHUNK, (ch + 1) * CHUNK)
            zc = z[:, ch * MLP_GROUP_DIM:(ch + 1) * MLP_GROUP_DIM] + sb_ref[:, cols]
            y_ref[rows, LRU_WIDTH + g * MLP_GROUP_DIM:LRU_WIDTH + (g + 1) * MLP_GROUP_DIM] = (
                u_ref[rows, cols] * zc).astype(BF16)

    h_sum = hf_ref[...].reshape(ts, LRU_WIDTH) + hb_ref[...]
    y_ref[:, 0:LRU_WIDTH] = (h_sum * gg_ref[...]).astype(BF16)


def _window_specs(seq, ts, reverse):
    nt = seq // ts
    rb = ts // SUBLANES
    sb = seq // SUBLANES

    def tidx(j):
        return nt - 1 - j if reverse else j

    return [
        pl.BlockSpec((ts, LRU_WIDTH), lambda b, j: (b * nt + tidx(j), 0)),
        pl.BlockSpec((SUBLANES, LRU_WIDTH),
                     lambda b, j: (b * sb + jnp.maximum(tidx(j) * rb - 1, 0), 0)),
        pl.BlockSpec((SUBLANES, LRU_WIDTH),
                     lambda b, j: (b * sb + jnp.minimum((tidx(j) + 1) * rb, sb - 1), 0)),
    ]


def _mix(xl, xc, gg, u, vn, seq, lru, h0_f, h0_b, sgu_w, sgu_bias, *, ts):
    conv_w, wg, b_r, b_i, lam = lru
    n = xl.shape[0]
    bsz = n // seq
    nt = seq // ts
    groups = ts // SUBLANES
    state_spec = pl.BlockSpec((1, 1, LRU_WIDTH), lambda b, j: (b, 0, 0))
    lru_specs = [
        _resident((4, LRU_WIDTH)),
        _resident((2, N_HALVES, HALF, 2 * HALF)),
        _resident((2, LRU_WIDTH)),
        _resident((2, LRU_WIDTH)),
        _resident((2, LRU_WIDTH)),
    ]
    lru_args = (conv_w, wg, b_r, b_i, lam)
    scan_scratch = [pltpu.VMEM((ts, LRU_WIDTH), F32)]
    carry_scratch = pltpu.VMEM((SUBLANES, LRU_WIDTH), F32)

    def tile_spec(width):
        return pl.BlockSpec((ts, width), lambda b, j: (b * nt + j, 0))

    hb = pl.pallas_call(
        functools.partial(_lru_bwd_kernel, ts=ts, nt=nt),
        grid=(bsz, nt),
        in_specs=_window_specs(seq, ts, True) + lru_specs + [state_spec],
        out_specs=pl.BlockSpec((groups, SUBLANES, LRU_WIDTH),
                               lambda b, j: (b * nt + nt - 1 - j, 0, 0)),
        out_shape=jax.ShapeDtypeStruct((n // SUBLANES, SUBLANES, LRU_WIDTH), F32),
        scratch_shapes=scan_scratch + [carry_scratch],
        compiler_params=_cparams(2),
        name="lru_bwd",
    )(xc, xl, xl, *lru_args, h0_b)
    hb = hb.reshape(n, LRU_WIDTH)

    y, h_last = pl.pallas_call(
        functools.partial(_mix_fwd_kernel, ts=ts, nt=nt),
        grid=(bsz, nt),
        in_specs=_window_specs(seq, ts, False) + [
            tile_spec(LRU_WIDTH), tile_spec(MLP_WIDTH), tile_spec(MLP_WIDTH), tile_spec(LRU_WIDTH),
        ] + lru_specs + [
            state_spec,
            _resident((MLP_GROUPS, CHUNK, CHUNK)),
            _resident((CHUNK, MLP_WIDTH)),
        ],
        out_specs=[
            tile_spec(D_MODEL),
            pl.BlockSpec((1, SUBLANES, LRU_WIDTH), lambda b, j: (b, 0, 0)),
        ],
        out_shape=[
            jax.ShapeDtypeStruct((n, D_MODEL), BF16),
            jax.ShapeDtypeStruct((bsz, SUBLANES, LRU_WIDTH), F32),
        ],
        scratch_shapes=scan_scratch + [
            pltpu.VMEM((groups, SUBLANES, LRU_WIDTH), F32),
            carry_scratch,
        ],
        compiler_params=_cparams(2),
        name="mix_fwd",
    )(xc, xl, xl, gg, u, vn, hb, *lru_args, h0_f, sgu_w, sgu_bias)
    return y, h_last[:, 0:1, :], hb


def _gate_weights(w_r, w_i):
    hpb = HALF // LRU_HEAD_DIM
    out = []
    for d in range(2):
        halves = []
        for hh in range(N_HALVES):
            blocks_r = [w_r[d, hh * hpb + k] for k in range(hpb)]
            blocks_i = [w_i[d, hh * hpb + k] for k in range(hpb)]
            halves.append(jnp.concatenate(
                [jax.scipy.linalg.block_diag(*blocks_r), jax.scipy.linalg.block_diag(*blocks_i)],
                axis=1))
        out.append(jnp.stack(halves))
    return (0.5 * jnp.stack(out)).astype(BF16)


def kernel(x, c, ctx, c_ctx, w_ada, b_ada, ffn1_norm_g, ffn1_w_in, ffn1_w_out,
           mix_norm_g, w_in_mix, lru_conv_w, lru_conv_b, lru_w_r, lru_b_r, lru_w_i,
           lru_b_i, lru_lambda, sgu_norm_g, sgu_w, sgu_b, w_out_mix,
           ffn2_norm_g, ffn2_w_in, ffn2_w_out, final_norm_g):
    bsz, n_lat, d = x.shape
    n_ctx = ctx.shape[1]
    depth = w_ada.shape[0]
    assert bsz + 1 <= MOD_ROWS and d == D_MODEL
    tm_lat, tm_ctx = 512, n_ctx

    c_rows = jnp.zeros((MOD_ROWS, d), F32).at[:bsz].set(c).at[bsz].set(c_ctx)
    mods = _ada(c_rows, w_ada, b_ada).reshape(depth, MOD_ROWS, N_MOD, d)

    h = x.reshape(bsz * n_lat, d)
    hc = ctx.reshape(bsz * n_ctx, d)
    zeros_state = jnp.zeros((bsz, 1, LRU_WIDTH), F32)

    for l in range(depth):
        last = l == depth - 1
        m = mods[l, :bsz]
        mc = mods[l, bsz:bsz + 1]
        w1_in, w1_out = ffn1_w_in[l].astype(BF16), ffn1_w_out[l].astype(BF16)
        w2_in, w2_out = ffn2_w_in[l].astype(BF16), ffn2_w_out[l].astype(BF16)
        w_mix_in, w_mix_out = w_in_mix[l].astype(BF16), w_out_mix[l].astype(BF16)
        lru = (lru_conv_w[l], _gate_weights(lru_w_r[l], lru_w_i[l]),
               lru_b_r[l], lru_b_i[l], lru_lambda[l])
        s_w = sgu_w[l].astype(BF16)
        s_bias = jnp.repeat(sgu_b[l].T, MLP_GROUP_DIM, axis=1)
        inproj = (mix_norm_g[l], w_mix_in, sgu_norm_g[l], lru_conv_w[l], lru_conv_b[l])

        hc, xl, xc, gg, u, vn = _ffn(hc, mc, 0, ffn1_norm_g[l], w1_in, w1_out, tm=tm_ctx,
                                 inproj=inproj)
        yc, hf_last, hb_c = _mix(xl, xc, gg, u, vn, n_ctx, lru, zeros_state, zeros_state,
                                 s_w, s_bias, ts=n_ctx)
        h0_b = hb_c.reshape(bsz, n_ctx, LRU_WIDTH)[:, 0:1, :]
        if not last:
            hc = _ffn(hc, mc, 6, ffn2_norm_g[l], w2_in, w2_out, tm=tm_ctx,
                      outproj=(yc, w_mix_out))

        h, xl, xc, gg, u, vn = _ffn(h, m, 0, ffn1_norm_g[l], w1_in, w1_out, tm=tm_lat,
                                inproj=inproj)
        y, _, _ = _mix(xl, xc, gg, u, vn, n_lat, lru, hf_last, h0_b, s_w, s_bias, ts=tm_lat)
        h = _ffn(h, m, 6, ffn2_norm_g[l], w2_in, w2_out, tm=2 * tm_lat, outproj=(y, w_mix_out),
                 final_g=final_norm_g if last else None)

    return h.reshape(bsz, n_lat, d)
```

```python
import functools

import jax
import jax.numpy as jnp
from jax import lax
from jax.experimental import pallas as pl
from jax.experimental.pallas import tpu as pltpu

F32 = jnp.float32
BF16 = jnp.bfloat16

EPS = 1e-6
D_MODEL = 1024
D_FF = 2816
LRU_WIDTH = 512
LRU_HEADS = 8
LRU_HEAD_DIM = 64
RG_C = 8.0
MLP_GROUPS = 4
MLP_GROUP_DIM = 128
MLP_WIDTH = 512
CHUNK = 128
IN_PROJ_WIDTH = 2048
N_MOD = 9
MOD_ROWS = 8
SUBLANES = 8
HALF = 256
N_HALVES = LRU_WIDTH // HALF
TINY = 1e-30
FFN_CHUNK = 256
CAST_BLOCK_ELEMS = 1024 * 1024

VMEM_LIMIT = 56 * 1024 * 1024


def _cparams(n_axes):
    return pltpu.CompilerParams(
        dimension_semantics=("arbitrary",) * n_axes,
        vmem_limit_bytes=VMEM_LIMIT,
    )


def _resident(shape, layer=None):
    nd = len(shape)
    if layer is None:
        return pl.BlockSpec(shape, lambda *_: (0,) * nd, pipeline_mode=pl.Buffered(1))
    return pl.BlockSpec((None,) + tuple(shape), lambda *_: (layer,) + (0,) * nd,
                        pipeline_mode=pl.Buffered(1))


def _cast_kernel(x_ref, o_ref):
    o_ref[...] = x_ref[...].astype(BF16)


def _to_bf16(w):
    depth, rows, cols = w.shape
    br = 8
    while br * 2 * cols <= CAST_BLOCK_ELEMS and rows % (br * 2) == 0:
        br *= 2
    spec = pl.BlockSpec((1, br, cols), lambda l, i: (l, i, 0))
    return pl.pallas_call(
        _cast_kernel,
        grid=(depth, rows // br),
        in_specs=[spec],
        out_specs=spec,
        out_shape=jax.ShapeDtypeStruct(w.shape, BF16),
        compiler_params=_cparams(2),
        name="to_bf16",
    )(w)


def _sigmoid(x):
    return 1.0 / (1.0 + jnp.exp(-x))


def _softplus(x):
    return jnp.maximum(x, 0.0) + jnp.log1p(jnp.exp(-jnp.abs(x)))


def _gelu_tanh(x):
    return 0.5 * x * (1.0 + jnp.tanh(0.7978845608028654 * (x + 0.044715 * (x * x * x))))


def _norm_mod(x, g, shift, scale):
    ms = jnp.mean(x * x, axis=-1, keepdims=True)
    return (x * lax.rsqrt(ms + EPS)) * (g * (1.0 + scale)) + shift


def _ada_kernel(c_ref, w_ref, b_ref, o_ref):
    c = c_ref[...]
    sc = c * _sigmoid(c)
    w = w_ref[0]
    s_hi = sc.astype(BF16)
    s_lo = (sc - s_hi.astype(F32)).astype(BF16)
    w_hi = w.astype(BF16)
    w_lo = (w - w_hi.astype(F32)).astype(BF16)
    p = jnp.dot(jnp.concatenate([s_hi, s_lo], axis=0), w_hi, preferred_element_type=F32)
    p = p[:MOD_ROWS] + p[MOD_ROWS:] + jnp.dot(s_hi, w_lo, preferred_element_type=F32)
    o_ref[0] = p + b_ref[0]


def _ada(c_rows, w_ada, b_ada):
    depth = w_ada.shape[0]
    return pl.pallas_call(
        _ada_kernel,
        grid=(depth, N_MOD),
        in_specs=[
            pl.BlockSpec((MOD_ROWS, D_MODEL), lambda l, j: (0, 0)),
            pl.BlockSpec((1, D_MODEL, D_MODEL), lambda l, j: (l, 0, j)),
            pl.BlockSpec((1, 1, D_MODEL), lambda l, j: (l, 0, j)),
        ],
        out_specs=pl.BlockSpec((1, MOD_ROWS, D_MODEL), lambda l, j: (l, 0, j)),
        out_shape=jax.ShapeDtypeStruct((depth, MOD_ROWS, N_MOD * D_MODEL), F32),
        compiler_params=_cparams(2),
        name="ada_mod",
    )(c_rows, w_ada, b_ada.reshape(depth, 1, N_MOD * D_MODEL))


def _ffn_kernel(*refs, k0, pro, epi):
    refs = list(refs)
    x_ref, mod_ref, g_ref, win_ref, wout_ref = refs[:5]
    pos = 5
    if pro == "outproj":
        y_ref, wo_ref = refs[pos:pos + 2]
        pos += 2
    if epi == "inproj":
        gm_ref, wmix_ref, sg_ref, cw_ref, cb_ref = refs[pos:pos + 5]
        pos += 5
    elif epi == "final":
        fg_ref = refs[pos]
        pos += 1
    o_ref = refs[pos]
    pos += 1
    if epi == "inproj":
        xl_ref, xc_ref, gg_ref, u_ref, vn_ref = refs[pos:pos + 5]
        pos += 5
    act_ref = refs[pos]
    pos += 1
    if epi == "inproj":
        xe_ref = refs[pos]
        tm = x_ref.shape[0]

        @pl.when(pl.program_id(0) == 0)
        def _():
            xe_ref[...] = jnp.zeros_like(xe_ref)

    h = x_ref[...]
    if pro == "outproj":
        h = h + mod_ref[0, 5:6, :] * jnp.dot(y_ref[...], wo_ref[...], preferred_element_type=F32)
    shift = mod_ref[0, k0:k0 + 1, :]
    scale = mod_ref[0, k0 + 1:k0 + 2, :]
    gate = mod_ref[0, k0 + 2:k0 + 3, :]
    zb = _norm_mod(h, g_ref[...], shift, scale).astype(BF16)
    for c0 in range(0, D_FF, FFN_CHUNK):
        cw = min(FFN_CHUNK, D_FF - c0)
        g = jnp.dot(zb, win_ref[:, c0:c0 + cw], preferred_element_type=F32)
        u = jnp.dot(zb, win_ref[:, D_FF + c0:D_FF + c0 + cw], preferred_element_type=F32)
        act_ref[:, c0:c0 + cw] = (g * _sigmoid(g) * u).astype(BF16)
    out = h + (0.5 * gate) * jnp.dot(act_ref[...], wout_ref[...], preferred_element_type=F32)
    if epi == "final":
        ms = jnp.mean(out * out, axis=-1, keepdims=True)
        out = out * lax.rsqrt(ms + EPS) * fg_ref[...]
    o_ref[...] = out
    if epi == "inproj":
        zm = _norm_mod(out, gm_ref[...], mod_ref[0, 3:4, :], mod_ref[0, 4:5, :]).astype(BF16)
        v = jnp.dot(zm, wmix_ref[:, 2 * LRU_WIDTH + MLP_WIDTH:], preferred_element_type=F32)
        vc = v - jnp.mean(v, axis=-1, keepdims=True)
        var = jnp.mean(vc * vc, axis=-1, keepdims=True)
        vn_ref[...] = (vc * lax.rsqrt(var + EPS) * sg_ref[...]).astype(BF16)
        xl = jnp.dot(zm, wmix_ref[:, 0:LRU_WIDTH], preferred_element_type=F32)
        xl_ref[...] = xl
        xe_ref[SUBLANES:SUBLANES + tm] = xl
        xc_ref[...] = _short_conv(xe_ref, cw_ref, cb_ref, tm, SUBLANES)
        gl = jnp.dot(zm, wmix_ref[:, LRU_WIDTH:2 * LRU_WIDTH], preferred_element_type=F32)
        gg_ref[...] = _gelu_tanh(gl)
        u_ref[...] = jnp.dot(zm, wmix_ref[:, 2 * LRU_WIDTH:2 * LRU_WIDTH + MLP_WIDTH],
                             preferred_element_type=F32)


def _ffn(x, mod, k0, norm_g, w_in, w_out, layer, *, tm, outproj=None, inproj=None,
         final_g=None):
    n = x.shape[0]
    tiles_per_mod = n // mod.shape[0] // tm
    pro = "outproj" if outproj is not None else None
    epi = "inproj" if inproj is not None else ("final" if final_g is not None else None)

    def row_spec(width):
        return pl.BlockSpec((tm, width), lambda i: (i, 0))

    args = [x, mod, norm_g.reshape(1, D_MODEL), w_in, w_out]
    in_specs = [
        row_spec(D_MODEL),
        pl.BlockSpec((1, N_MOD, D_MODEL), lambda i: (i // tiles_per_mod, 0, 0)),
        _resident((1, D_MODEL)),
        _resident((D_MODEL, 2 * D_FF), layer),
        _resident((D_FF, D_MODEL), layer),
    ]
    out_shape = [jax.ShapeDtypeStruct((n, D_MODEL), F32)]
    out_specs = [row_spec(D_MODEL)]
    if pro == "outproj":
        args += [outproj[0], outproj[1]]
        in_specs += [row_spec(D_MODEL), _resident((D_MODEL, D_MODEL), layer)]
    scratch = [pltpu.VMEM((tm, D_FF), BF16)]
    if epi == "inproj":
        args += [inproj[0].reshape(1, D_MODEL), inproj[1], inproj[2].reshape(1, MLP_WIDTH),
                 inproj[3], inproj[4].reshape(1, LRU_WIDTH)]
        in_specs += [_resident((1, D_MODEL)), _resident((D_MODEL, IN_PROJ_WIDTH), layer),
                     _resident((1, MLP_WIDTH)), _resident((4, LRU_WIDTH)),
                     _resident((1, LRU_WIDTH))]
        out_shape += [jax.ShapeDtypeStruct((n, LRU_WIDTH), F32),
                      jax.ShapeDtypeStruct((n, LRU_WIDTH), F32),
                      jax.ShapeDtypeStruct((n, LRU_WIDTH), F32),
                      jax.ShapeDtypeStruct((n, MLP_WIDTH), F32),
                      jax.ShapeDtypeStruct((n, MLP_WIDTH), BF16)]
        out_specs += [row_spec(LRU_WIDTH), row_spec(LRU_WIDTH), row_spec(LRU_WIDTH),
                      row_spec(MLP_WIDTH), row_spec(MLP_WIDTH)]
        scratch += [pltpu.VMEM((tm + 2 * SUBLANES, LRU_WIDTH), F32)]
    elif epi == "final":
        args += [final_g.reshape(1, D_MODEL)]
        in_specs += [_resident((1, D_MODEL))]
    res = pl.pallas_call(
        functools.partial(_ffn_kernel, k0=k0, pro=pro, epi=epi),
        grid=(n // tm,),
        in_specs=in_specs,
        out_specs=out_specs,
        out_shape=out_shape,
        scratch_shapes=scratch,
        compiler_params=_cparams(1),
        name="ffn_" + (pro or "plain") + "_" + (epi or "plain"),
    )(*args)
    return res if epi == "inproj" else res[0]


def _short_conv(xe_ref, cw_ref, cb_ref, ts, halo):
    xe = xe_ref[...]
    rows = ts + 2 * halo
    xc = cb_ref[...] + xe[halo:halo + ts] * cw_ref[2:3, :]
    for k, sh in ((0, 2), (1, 1), (3, rows - 1)):
        xc = xc + pltpu.roll(xe, sh, 0)[halo:halo + ts] * cw_ref[k:k + 1, :]
    return xc


def _finish_conv(xc_ref, xl_ref, xp_ref, xn_ref, cw_ref, xcs_ref, tile, nt, ts, tc):
    zeros = jnp.zeros((SUBLANES, LRU_WIDTH), F32)
    xcs_ref[...] = xc_ref[...]
    for r0 in range(0, ts, tc):
        r1 = r0 + tc
        before = jnp.where(tile > 0, xp_ref[...], 0.0) if r0 == 0 else xl_ref[r0 - SUBLANES:r0]
        after = jnp.where(tile < nt - 1, xn_ref[...], 0.0) if r1 == ts else xl_ref[r1:r1 + SUBLANES]
        e = jnp.concatenate([before, zeros], axis=0)
        head = (pltpu.roll(e, 2, 0)[SUBLANES:] * cw_ref[0:1, :]
                + pltpu.roll(e, 1, 0)[SUBLANES:] * cw_ref[1:2, :])
        f = jnp.concatenate([zeros, after], axis=0)
        tail = pltpu.roll(f, 2 * SUBLANES - 1, 0)[0:SUBLANES] * cw_ref[3:4, :]
        xcs_ref[r0:r0 + SUBLANES] = xcs_ref[r0:r0 + SUBLANES] + head
        xcs_ref[r1 - SUBLANES:r1] = xcs_ref[r1 - SUBLANES:r1] + tail


def _lru_scan(d, xc, wg_ref, br_ref, bi_ref, lam_ref, h_ref, g0, carry, ts):
    groups = ts // SUBLANES
    sub = lax.broadcasted_iota(jnp.int32, (groups, SUBLANES, HALF), 1)
    order = range(groups) if d == 0 else range(groups - 1, -1, -1)
    carry_out = []
    for hh in range(N_HALVES):
        cols = slice(hh * HALF, (hh + 1) * HALF)
        xch = xc[:, cols]
        kk = (-0.5 * RG_C) * _softplus(-lam_ref[d:d + 1, cols])
        pre = jnp.dot(xch.astype(BF16), wg_ref[d, hh], preferred_element_type=F32)
        tr = jnp.tanh(pre[:, :HALF] + 0.5 * br_ref[d:d + 1, cols])
        ti = jnp.tanh(pre[:, HALF:] + 0.5 * bi_ref[d:d + 1, cols])
        log_a = kk + kk * tr
        a = jnp.exp(log_a)
        t = jnp.tanh(log_a)
        q = t / (t - 1.0)
        gain = q * lax.rsqrt(jnp.maximum(q, TINY))
        bc = gain * ((0.7071067811865476 * xch) * (1.0 + ti))
        a3 = a.reshape(groups, SUBLANES, HALF)
        b3 = bc.reshape(groups, SUBLANES, HALF)
        for s in (1, 2, 4):
            if d == 0:
                sh, m = s, sub >= s
            else:
                sh, m = SUBLANES - s, sub < SUBLANES - s
            am = jnp.where(m, a3, 0.0)
            b3 = b3 + am * pltpu.roll(b3, sh, 1)
            a3 = jnp.where(m, a3 * pltpu.roll(a3, sh, 1), a3)
        c = carry[:, cols]
        for g in order:
            h = a3[g] * c + b3[g]
            h_ref[g0 + g, :, cols] = h
            row = h[SUBLANES - 1:SUBLANES, :] if d == 0 else h[0:1, :]
            c = jnp.broadcast_to(row, (SUBLANES, HALF))
        carry_out.append(c)
    return jnp.concatenate(carry_out, axis=1)


def _lru_bwd_kernel(xc_ref, xl_ref, xp_ref, xn_ref, cw_ref, wg_ref, br_ref, bi_ref, lam_ref,
                    h0_ref, hb_ref, xcs_ref, carry_ref, *, ts, nt, tc):
    j = pl.program_id(1)

    @pl.when(j == 0)
    def _():
        carry_ref[...] = jnp.broadcast_to(h0_ref[0], (SUBLANES, LRU_WIDTH))

    _finish_conv(xc_ref, xl_ref, xp_ref, xn_ref, cw_ref, xcs_ref, nt - 1 - j, nt, ts, tc)
    carry = carry_ref[...]
    for r0 in range(ts - tc, -1, -tc):
        carry = _lru_scan(1, xcs_ref[r0:r0 + tc], wg_ref, br_ref, bi_ref, lam_ref, hb_ref,
                          r0 // SUBLANES, carry, tc)
    carry_ref[...] = carry


def _mix_fwd_kernel(xc_ref, xl_ref, xp_ref, xn_ref, gg_ref, u_ref, vn_ref, hb_ref,
                    cw_ref, wg_ref, br_ref, bi_ref, lam_ref, h0_ref, sw_ref, sb_ref,
                    y_ref, hlast_ref, xcs_ref, hf_ref, carry_ref, *, ts, nt, tc):
    j = pl.program_id(1)
    n_chunks = ts // CHUNK

    @pl.when(j == 0)
    def _():
        carry_ref[...] = jnp.broadcast_to(h0_ref[0], (SUBLANES, LRU_WIDTH))

    _finish_conv(xc_ref, xl_ref, xp_ref, xn_ref, cw_ref, xcs_ref, j, nt, ts, tc)
    carry = carry_ref[...]
    for r0 in range(0, ts, tc):
        carry = _lru_scan(0, xcs_ref[r0:r0 + tc], wg_ref, br_ref, bi_ref, lam_ref, hf_ref,
                          r0 // SUBLANES, carry, tc)
    carry_ref[...] = carry
    hlast_ref[0] = carry
    for g in range(MLP_GROUPS):
        cols = slice(g * MLP_GROUP_DIM, (g + 1) * MLP_GROUP_DIM)
        rhs = jnp.concatenate(
            [vn_ref[ch * CHUNK:(ch + 1) * CHUNK, cols] for ch in range(n_chunks)], axis=1)
        z = jnp.dot(sw_ref[g], rhs, preferred_element_type=F32)
        for ch in range(n_chunks):
            rows = slice(ch * CHUNK, (ch + 1) * CHUNK)
            zc = z[:, ch * MLP_GROUP_DIM:(ch + 1) * MLP_GROUP_DIM] + sb_ref[:, cols]
            y_ref[rows, LRU_WIDTH + g * MLP_GROUP_DIM:LRU_WIDTH + (g + 1) * MLP_GROUP_DIM] = (
                u_ref[rows, cols] * zc).astype(BF16)

    h_sum = hf_ref[...].reshape(ts, LRU_WIDTH) + hb_ref[...]
    y_ref[:, 0:LRU_WIDTH] = (h_sum * gg_ref[...]).astype(BF16)


def _window_specs(seq, ts, reverse):
    nt = seq // ts
    rb = ts // SUBLANES
    sb = seq // SUBLANES

    def tidx(j):
        return nt - 1 - j if reverse else j

    return [
        pl.BlockSpec((ts, LRU_WIDTH), lambda b, j: (b * nt + tidx(j), 0)),
        pl.BlockSpec((ts, LRU_WIDTH), lambda b, j: (b * nt + tidx(j), 0)),
        pl.BlockSpec((SUBLANES, LRU_WIDTH),
                     lambda b, j: (b * sb + jnp.maximum(tidx(j) * rb - 1, 0), 0)),
        pl.BlockSpec((SUBLANES, LRU_WIDTH),
                     lambda b, j: (b * sb + jnp.minimum((tidx(j) + 1) * rb, sb - 1), 0)),
    ]


def _mix(xl, xc, gg, u, vn, seq, lru, h0_f, h0_b, sgu_w, sgu_bias, *, ts, tc):
    conv_w, wg, b_r, b_i, lam = lru
    n = xl.shape[0]
    bsz = n // seq
    nt = seq // ts
    groups = ts // SUBLANES
    state_spec = pl.BlockSpec((1, 1, LRU_WIDTH), lambda b, j: (b, 0, 0))
    lru_specs = [
        _resident((4, LRU_WIDTH)),
        _resident((2, N_HALVES, HALF, 2 * HALF)),
        _resident((2, LRU_WIDTH)),
        _resident((2, LRU_WIDTH)),
        _resident((2, LRU_WIDTH)),
    ]
    lru_args = (conv_w, wg, b_r, b_i, lam)
    scan_scratch = [pltpu.VMEM((ts, LRU_WIDTH), F32)]
    carry_scratch = pltpu.VMEM((SUBLANES, LRU_WIDTH), F32)

    def tile_spec(width):
        return pl.BlockSpec((ts, width), lambda b, j: (b * nt + j, 0))

    hb = pl.pallas_call(
        functools.partial(_lru_bwd_kernel, ts=ts, nt=nt, tc=tc),
        grid=(bsz, nt),
        in_specs=_window_specs(seq, ts, True) + lru_specs + [state_spec],
        out_specs=pl.BlockSpec((groups, SUBLANES, LRU_WIDTH),
                               lambda b, j: (b * nt + nt - 1 - j, 0, 0)),
        out_shape=jax.ShapeDtypeStruct((n // SUBLANES, SUBLANES, LRU_WIDTH), F32),
        scratch_shapes=scan_scratch + [carry_scratch],
        compiler_params=_cparams(2),
        name="lru_bwd",
    )(xc, xl, xl, xl, *lru_args, h0_b)
    hb = hb.reshape(n, LRU_WIDTH)

    y, h_last = pl.pallas_call(
        functools.partial(_mix_fwd_kernel, ts=ts, nt=nt, tc=tc),
        grid=(bsz, nt),
        in_specs=_window_specs(seq, ts, False) + [
            tile_spec(LRU_WIDTH), tile_spec(MLP_WIDTH), tile_spec(MLP_WIDTH), tile_spec(LRU_WIDTH),
        ] + lru_specs + [
            state_spec,
            _resident((MLP_GROUPS, CHUNK, CHUNK)),
            _resident((CHUNK, MLP_WIDTH)),
        ],
        out_specs=[
            tile_spec(D_MODEL),
            pl.BlockSpec((1, SUBLANES, LRU_WIDTH), lambda b, j: (b, 0, 0)),
        ],
        out_shape=[
            jax.ShapeDtypeStruct((n, D_MODEL), BF16),
            jax.ShapeDtypeStruct((bsz, SUBLANES, LRU_WIDTH), F32),
        ],
        scratch_shapes=scan_scratch + [
            pltpu.VMEM((groups, SUBLANES, LRU_WIDTH), F32),
            carry_scratch,
        ],
        compiler_params=_cparams(2),
        name="mix_fwd",
    )(xc, xl, xl, xl, gg, u, vn, hb, *lru_args, h0_f, sgu_w, sgu_bias)
    return y, h_last[:, 0:1, :], hb


def _gate_weights(w_r, w_i):
    hpb = HALF // LRU_HEAD_DIM
    out = []
    for d in range(2):
        halves = []
        for hh in range(N_HALVES):
            blocks_r = [w_r[d, hh * hpb + k] for k in range(hpb)]
            blocks_i = [w_i[d, hh * hpb + k] for k in range(hpb)]
            halves.append(jnp.concatenate(
                [jax.scipy.linalg.block_diag(*blocks_r), jax.scipy.linalg.block_diag(*blocks_i)],
                axis=1))
        out.append(jnp.stack(halves))
    return (0.5 * jnp.stack(out)).astype(BF16)


def kernel(x, c, ctx, c_ctx, w_ada, b_ada, ffn1_norm_g, ffn1_w_in, ffn1_w_out,
           mix_norm_g, w_in_mix, lru_conv_w, lru_conv_b, lru_w_r, lru_b_r, lru_w_i,
           lru_b_i, lru_lambda, sgu_norm_g, sgu_w, sgu_b, w_out_mix,
           ffn2_norm_g, ffn2_w_in, ffn2_w_out, final_norm_g):
    bsz, n_lat, d = x.shape
    n_ctx = ctx.shape[1]
    depth = w_ada.shape[0]
    assert bsz + 1 <= MOD_ROWS and d == D_MODEL
    tm_lat, tm_ctx = 512, n_ctx

    c_rows = jnp.zeros((MOD_ROWS, d), F32).at[:bsz].set(c).at[bsz].set(c_ctx)
    mods = _ada(c_rows, w_ada, b_ada).reshape(depth, MOD_ROWS, N_MOD, d)

    h = x.reshape(bsz * n_lat, d)
    hc = ctx.reshape(bsz * n_ctx, d)
    zeros_state = jnp.zeros((bsz, 1, LRU_WIDTH), F32)
    w1_in, w1_out = _to_bf16(ffn1_w_in), _to_bf16(ffn1_w_out)
    w2_in, w2_out = _to_bf16(ffn2_w_in), _to_bf16(ffn2_w_out)
    w_mix_in, w_mix_out = _to_bf16(w_in_mix), _to_bf16(w_out_mix)

    for l in range(depth):
        last = l == depth - 1
        m = mods[l, :bsz]
        mc = mods[l, bsz:bsz + 1]
        lru = (lru_conv_w[l], _gate_weights(lru_w_r[l], lru_w_i[l]),
               lru_b_r[l], lru_b_i[l], lru_lambda[l])
        s_w = sgu_w[l].astype(BF16)
        s_bias = jnp.repeat(sgu_b[l].T, MLP_GROUP_DIM, axis=1)
        inproj = (mix_norm_g[l], w_mix_in, sgu_norm_g[l], lru_conv_w[l], lru_conv_b[l])

        hc, xl, xc, gg, u, vn = _ffn(hc, mc, 0, ffn1_norm_g[l], w1_in, w1_out, l, tm=tm_ctx,
                                 inproj=inproj)
        yc, hf_last, hb_c = _mix(xl, xc, gg, u, vn, n_ctx, lru, zeros_state, zeros_state,
                                 s_w, s_bias, ts=n_ctx, tc=tm_ctx)
        h0_b = hb_c.reshape(bsz, n_ctx, LRU_WIDTH)[:, 0:1, :]
        if not last:
            hc = _ffn(hc, mc, 6, ffn2_norm_g[l], w2_in, w2_out, l, tm=tm_ctx,
                      outproj=(yc, w_mix_out))

        h, xl, xc, gg, u, vn = _ffn(h, m, 0, ffn1_norm_g[l], w1_in, w1_out, l, tm=tm_lat,
                                inproj=inproj)
        y, _, _ = _mix(xl, xc, gg, u, vn, n_lat, lru, hf_last, h0_b, s_w, s_bias,
                       ts=2 * tm_lat, tc=tm_lat)
        h = _ffn(h, m, 6, ffn2_norm_g[l], w2_in, w2_out, l, tm=2 * tm_lat,
                 outproj=(y, w_mix_out), final_g=final_norm_g if last else None)

    return h.reshape(bsz, n_lat, d)
```

```python
import functools

import jax
import jax.numpy as jnp
from jax import lax
from jax.experimental import pallas as pl
from jax.experimental.pallas import tpu as pltpu

F32 = jnp.float32
BF16 = jnp.bfloat16

EPS = 1e-6
D_MODEL = 1024
D_FF = 2816
LRU_WIDTH = 512
LRU_HEADS = 8
LRU_HEAD_DIM = 64
RG_C = 8.0
MLP_GROUPS = 4
MLP_GROUP_DIM = 128
MLP_WIDTH = 512
CHUNK = 128
IN_PROJ_WIDTH = 2048
N_MOD = 9
MOD_ROWS = 8
SUBLANES = 8
HALF = 256
N_HALVES = LRU_WIDTH // HALF
TINY = 1e-30
FFN_CHUNK = 256
CAST_BLOCK_ELEMS = 2 * 1024 * 1024

VMEM_LIMIT = 56 * 1024 * 1024


def _cparams(n_axes):
    return pltpu.CompilerParams(
        dimension_semantics=("arbitrary",) * n_axes,
        vmem_limit_bytes=VMEM_LIMIT,
    )


def _resident(shape, layer=None):
    nd = len(shape)
    if layer is None:
        return pl.BlockSpec(shape, lambda *_: (0,) * nd, pipeline_mode=pl.Buffered(1))
    return pl.BlockSpec((None,) + tuple(shape), lambda *_: (layer,) + (0,) * nd,
                        pipeline_mode=pl.Buffered(1))


def _cast_kernel(x_ref, o_ref):
    o_ref[...] = x_ref[...].astype(BF16)


def _to_bf16(w):
    depth, rows, cols = w.shape
    br = max(b for b in range(16, rows + 1, 16)
             if rows % b == 0 and b * cols <= CAST_BLOCK_ELEMS)
    spec = pl.BlockSpec((1, br, cols), lambda l, i: (l, i, 0))
    return pl.pallas_call(
        _cast_kernel,
        grid=(depth, rows // br),
        in_specs=[spec],
        out_specs=spec,
        out_shape=jax.ShapeDtypeStruct(w.shape, BF16),
        compiler_params=_cparams(2),
        name="to_bf16",
    )(w)


def _sigmoid(x):
    return 1.0 / (1.0 + jnp.exp(-x))


def _softplus(x):
    return jnp.maximum(x, 0.0) + jnp.log1p(jnp.exp(-jnp.abs(x)))


def _gelu_tanh(x):
    return 0.5 * x * (1.0 + jnp.tanh(0.7978845608028654 * (x + 0.044715 * (x * x * x))))


def _norm_mod(x, g, shift, scale):
    ms = jnp.mean(x * x, axis=-1, keepdims=True)
    return (x * lax.rsqrt(ms + EPS)) * (g * (1.0 + scale)) + shift


def _ada_kernel(c_ref, w_ref, b_ref, o_ref):
    c = c_ref[...]
    sc = c * _sigmoid(c)
    w = w_ref[0]
    s_hi = sc.astype(BF16)
    s_lo = (sc - s_hi.astype(F32)).astype(BF16)
    w_hi = w.astype(BF16)
    w_lo = (w - w_hi.astype(F32)).astype(BF16)
    p = jnp.dot(jnp.concatenate([s_hi, s_lo], axis=0), w_hi, preferred_element_type=F32)
    p = p[:MOD_ROWS] + p[MOD_ROWS:] + jnp.dot(s_hi, w_lo, preferred_element_type=F32)
    o_ref[0] = p + b_ref[0]


def _ada(c_rows, w_ada, b_ada):
    depth = w_ada.shape[0]
    return pl.pallas_call(
        _ada_kernel,
        grid=(depth, N_MOD),
        in_specs=[
            pl.BlockSpec((MOD_ROWS, D_MODEL), lambda l, j: (0, 0)),
            pl.BlockSpec((1, D_MODEL, D_MODEL), lambda l, j: (l, 0, j)),
            pl.BlockSpec((1, 1, D_MODEL), lambda l, j: (l, 0, j)),
        ],
        out_specs=pl.BlockSpec((1, MOD_ROWS, D_MODEL), lambda l, j: (l, 0, j)),
        out_shape=jax.ShapeDtypeStruct((depth, MOD_ROWS, N_MOD * D_MODEL), F32),
        compiler_params=_cparams(2),
        name="ada_mod",
    )(c_rows, w_ada, b_ada.reshape(depth, 1, N_MOD * D_MODEL))


def _ffn_kernel(*refs, k0, pro, epi):
    refs = list(refs)
    x_ref, mod_ref, g_ref, win_ref, wout_ref = refs[:5]
    pos = 5
    if pro == "outproj":
        y_ref, wo_ref = refs[pos:pos + 2]
        pos += 2
    if epi == "inproj":
        gm_ref, wmix_ref, sg_ref, cw_ref, cb_ref = refs[pos:pos + 5]
        pos += 5
    elif epi == "final":
        fg_ref = refs[pos]
        pos += 1
    o_ref = refs[pos]
    pos += 1
    if epi == "inproj":
        xl_ref, xc_ref, gg_ref, u_ref, vn_ref = refs[pos:pos + 5]
        pos += 5
    act_ref = refs[pos]
    pos += 1
    if epi == "inproj":
        xe_ref = refs[pos]
        tm = x_ref.shape[0]

        @pl.when(pl.program_id(0) == 0)
        def _():
            xe_ref[...] = jnp.zeros_like(xe_ref)

    h = x_ref[...]
    if pro == "outproj":
        h = h + mod_ref[0, 5:6, :] * jnp.dot(y_ref[...], wo_ref[...], preferred_element_type=F32)
    shift = mod_ref[0, k0:k0 + 1, :]
    scale = mod_ref[0, k0 + 1:k0 + 2, :]
    gate = mod_ref[0, k0 + 2:k0 + 3, :]
    zb = _norm_mod(h, g_ref[...], shift, scale).astype(BF16)
    for c0 in range(0, D_FF, FFN_CHUNK):
        cw = min(FFN_CHUNK, D_FF - c0)
        g = jnp.dot(zb, win_ref[:, c0:c0 + cw], preferred_element_type=F32)
        u = jnp.dot(zb, win_ref[:, D_FF + c0:D_FF + c0 + cw], preferred_element_type=F32)
        act_ref[:, c0:c0 + cw] = (g * _sigmoid(g) * u).astype(BF16)
    out = h + (0.5 * gate) * jnp.dot(act_ref[...], wout_ref[...], preferred_element_type=F32)
    if epi == "final":
        ms = jnp.mean(out * out, axis=-1, keepdims=True)
        out = out * lax.rsqrt(ms + EPS) * fg_ref[...]
    o_ref[...] = out
    if epi == "inproj":
        zm = _norm_mod(out, gm_ref[...], mod_ref[0, 3:4, :], mod_ref[0, 4:5, :]).astype(BF16)
        v = jnp.dot(zm, wmix_ref[:, 2 * LRU_WIDTH + MLP_WIDTH:], preferred_element_type=F32)
        vc = v - jnp.mean(v, axis=-1, keepdims=True)
        var = jnp.mean(vc * vc, axis=-1, keepdims=True)
        vn_ref[...] = (vc * lax.rsqrt(var + EPS) * sg_ref[...]).astype(BF16)
        xl = jnp.dot(zm, wmix_ref[:, 0:LRU_WIDTH], preferred_element_type=F32)
        xl_ref[...] = xl
        xe_ref[SUBLANES:SUBLANES + tm] = xl
        xc_ref[...] = _short_conv(xe_ref, cw_ref, cb_ref, tm, SUBLANES)
        gl = jnp.dot(zm, wmix_ref[:, LRU_WIDTH:2 * LRU_WIDTH], preferred_element_type=F32)
        gg_ref[...] = _gelu_tanh(gl)
        u_ref[...] = jnp.dot(zm, wmix_ref[:, 2 * LRU_WIDTH:2 * LRU_WIDTH + MLP_WIDTH],
                             preferred_element_type=F32)


def _ffn(x, mod, k0, norm_g, w_in, w_out, layer, *, tm, outproj=None, inproj=None,
         final_g=None):
    n = x.shape[0]
    tiles_per_mod = n // mod.shape[0] // tm
    pro = "outproj" if outproj is not None else None
    epi = "inproj" if inproj is not None else ("final" if final_g is not None else None)

    def row_spec(width):
        return pl.BlockSpec((tm, width), lambda i: (i, 0))

    args = [x, mod, norm_g.reshape(1, D_MODEL), w_in, w_out]
    in_specs = [
        row_spec(D_MODEL),
        pl.BlockSpec((1, N_MOD, D_MODEL), lambda i: (i // tiles_per_mod, 0, 0)),
        _resident((1, D_MODEL)),
        _resident((D_MODEL, 2 * D_FF), layer),
        _resident((D_FF, D_MODEL), layer),
    ]
    out_shape = [jax.ShapeDtypeStruct((n, D_MODEL), F32)]
    out_specs = [row_spec(D_MODEL)]
    if pro == "outproj":
        args += [outproj[0], outproj[1]]
        in_specs += [row_spec(D_MODEL), _resident((D_MODEL, D_MODEL), layer)]
    scratch = [pltpu.VMEM((tm, D_FF), BF16)]
    if epi == "inproj":
        args += [inproj[0].reshape(1, D_MODEL), inproj[1], inproj[2].reshape(1, MLP_WIDTH),
                 inproj[3], inproj[4].reshape(1, LRU_WIDTH)]
        in_specs += [_resident((1, D_MODEL)), _resident((D_MODEL, IN_PROJ_WIDTH), layer),
                     _resident((1, MLP_WIDTH)), _resident((4, LRU_WIDTH)),
                     _resident((1, LRU_WIDTH))]
        out_shape += [jax.ShapeDtypeStruct((n, LRU_WIDTH), F32),
                      jax.ShapeDtypeStruct((n, LRU_WIDTH), F32),
                      jax.ShapeDtypeStruct((n, LRU_WIDTH), F32),
                      jax.ShapeDtypeStruct((n, MLP_WIDTH), F32),
                      jax.ShapeDtypeStruct((n, MLP_WIDTH), BF16)]
        out_specs += [row_spec(LRU_WIDTH), row_spec(LRU_WIDTH), row_spec(LRU_WIDTH),
                      row_spec(MLP_WIDTH), row_spec(MLP_WIDTH)]
        scratch += [pltpu.VMEM((tm + 2 * SUBLANES, LRU_WIDTH), F32)]
    elif epi == "final":
        args += [final_g.reshape(1, D_MODEL)]
        in_specs += [_resident((1, D_MODEL))]
    res = pl.pallas_call(
        functools.partial(_ffn_kernel, k0=k0, pro=pro, epi=epi),
        grid=(n // tm,),
        in_specs=in_specs,
        out_specs=out_specs,
        out_shape=out_shape,
        scratch_shapes=scratch,
        compiler_params=_cparams(1),
        name="ffn_" + (pro or "plain") + "_" + (epi or "plain"),
    )(*args)
    return res if epi == "inproj" else res[0]


def _short_conv(xe_ref, cw_ref, cb_ref, ts, halo):
    xe = xe_ref[...]
    rows = ts + 2 * halo
    xc = cb_ref[...] + xe[halo:halo + ts] * cw_ref[2:3, :]
    for k, sh in ((0, 2), (1, 1), (3, rows - 1)):
        xc = xc + pltpu.roll(xe, sh, 0)[halo:halo + ts] * cw_ref[k:k + 1, :]
    return xc


def _finish_conv(xc_ref, xl_ref, xp_ref, xn_ref, cw_ref, xcs_ref, tile, nt, ts, tc):
    zeros = jnp.zeros((SUBLANES, LRU_WIDTH), F32)
    xcs_ref[...] = xc_ref[...]
    for r0 in range(0, ts, tc):
        r1 = r0 + tc
        before = jnp.where(tile > 0, xp_ref[...], 0.0) if r0 == 0 else xl_ref[r0 - SUBLANES:r0]
        after = jnp.where(tile < nt - 1, xn_ref[...], 0.0) if r1 == ts else xl_ref[r1:r1 + SUBLANES]
        e = jnp.concatenate([before, zeros], axis=0)
        head = (pltpu.roll(e, 2, 0)[SUBLANES:] * cw_ref[0:1, :]
                + pltpu.roll(e, 1, 0)[SUBLANES:] * cw_ref[1:2, :])
        f = jnp.concatenate([zeros, after], axis=0)
        tail = pltpu.roll(f, 2 * SUBLANES - 1, 0)[0:SUBLANES] * cw_ref[3:4, :]
        xcs_ref[r0:r0 + SUBLANES] = xcs_ref[r0:r0 + SUBLANES] + head
        xcs_ref[r1 - SUBLANES:r1] = xcs_ref[r1 - SUBLANES:r1] + tail


def _lru_scan(d, xc, wg_ref, br_ref, bi_ref, lam_ref, h_ref, g0, carry, ts):
    groups = ts // SUBLANES
    sub = lax.broadcasted_iota(jnp.int32, (groups, SUBLANES, HALF), 1)
    order = range(groups) if d == 0 else range(groups - 1, -1, -1)
    row_id = lax.broadcasted_iota(jnp.int32, (SUBLANES, HALF), 0)
    near = row_id < SUBLANES // 2 if d == 0 else row_id >= SUBLANES // 2
    carry_out = []
    for hh in range(N_HALVES):
        cols = slice(hh * HALF, (hh + 1) * HALF)
        xch = xc[:, cols]
        kk = (-0.5 * RG_C) * _softplus(-lam_ref[d:d + 1, cols])
        pre = jnp.dot(xch.astype(BF16), wg_ref[d, hh], preferred_element_type=F32)
        tr = jnp.tanh(pre[:, :HALF] + 0.5 * br_ref[d:d + 1, cols])
        ti = jnp.tanh(pre[:, HALF:] + 0.5 * bi_ref[d:d + 1, cols])
        log_a = kk + kk * tr
        a = jnp.exp(log_a)
        t = jnp.tanh(log_a)
        q = t / (t - 1.0)
        gain = q * lax.rsqrt(jnp.maximum(q, TINY))
        bc = gain * ((0.7071067811865476 * xch) * (1.0 + ti))
        a3 = a.reshape(groups, SUBLANES, HALF)
        b3 = bc.reshape(groups, SUBLANES, HALF)
        for s in (1, 2):
            if d == 0:
                sh, m = s, sub >= s
            else:
                sh, m = SUBLANES - s, sub < SUBLANES - s
            am = jnp.where(m, a3, 0.0)
            b3 = b3 + am * pltpu.roll(b3, sh, 1)
            a3 = jnp.where(m, a3 * pltpu.roll(a3, sh, 1), a3)
        c = carry[:, cols]
        for g in order:
            h_near = a3[g] * c + b3[g]
            h_far = a3[g] * pltpu.roll(h_near, SUBLANES // 2, 0) + b3[g]
            h = jnp.where(near, h_near, h_far)
            h_ref[g0 + g, :, cols] = h
            row = h[SUBLANES - 1:SUBLANES, :] if d == 0 else h[0:1, :]
            c = jnp.broadcast_to(row, (SUBLANES, HALF))
        carry_out.append(c)
    return jnp.concatenate(carry_out, axis=1)


def _lru_bwd_kernel(xc_ref, xl_ref, xp_ref, xn_ref, cw_ref, wg_ref, br_ref, bi_ref, lam_ref,
                    h0_ref, hb_ref, xcs_ref, carry_ref, *, ts, nt, tc):
    j = pl.program_id(1)

    @pl.when(j == 0)
    def _():
        carry_ref[...] = jnp.broadcast_to(h0_ref[0], (SUBLANES, LRU_WIDTH))

    _finish_conv(xc_ref, xl_ref, xp_ref, xn_ref, cw_ref, xcs_ref, nt - 1 - j, nt, ts, tc)
    carry = carry_ref[...]
    for r0 in range(ts - tc, -1, -tc):
        carry = _lru_scan(1, xcs_ref[r0:r0 + tc], wg_ref, br_ref, bi_ref, lam_ref, hb_ref,
                          r0 // SUBLANES, carry, tc)
    carry_ref[...] = carry


def _mix_fwd_kernel(xc_ref, xl_ref, xp_ref, xn_ref, gg_ref, u_ref, vn_ref, hb_ref,
                    cw_ref, wg_ref, br_ref, bi_ref, lam_ref, h0_ref, sw_ref, sb_ref,
                    y_ref, hlast_ref, xcs_ref, hf_ref, carry_ref, *, ts, nt, tc):
    j = pl.program_id(1)
    n_chunks = ts // CHUNK

    @pl.when(j == 0)
    def _():
        carry_ref[...] = jnp.broadcast_to(h0_ref[0], (SUBLANES, LRU_WIDTH))

    _finish_conv(xc_ref, xl_ref, xp_ref, xn_ref, cw_ref, xcs_ref, j, nt, ts, tc)
    carry = carry_ref[...]
    for r0 in range(0, ts, tc):
        carry = _lru_scan(0, xcs_ref[r0:r0 + tc], wg_ref, br_ref, bi_ref, lam_ref, hf_ref,
                          r0 // SUBLANES, carry, tc)
    carry_ref[...] = carry
    hlast_ref[0] = carry
    for g in range(MLP_GROUPS):
        cols = slice(g * MLP_GROUP_DIM, (g + 1) * MLP_GROUP_DIM)
        rhs = jnp.concatenate(
            [vn_ref[ch * CHUNK:(ch + 1) * CHUNK, cols] for ch in range(n_chunks)], axis=1)
        z = jnp.dot(sw_ref[g], rhs, preferred_element_type=F32)
        for ch in range(n_chunks):
            rows = slice(ch * CHUNK, (ch + 1) * CHUNK)
            zc = z[:, ch * MLP_GROUP_DIM:(ch + 1) * MLP_GROUP_DIM] + sb_ref[:, cols]
            y_ref[rows, LRU_WIDTH + g * MLP_GROUP_DIM:LRU_WIDTH + (g + 1) * MLP_GROUP_DIM] = (
                u_ref[rows, cols] * zc).astype(BF16)

    h_sum = hf_ref[...].reshape(ts, LRU_WIDTH) + hb_ref[...]
    y_ref[:, 0:LRU_WIDTH] = (h_sum * gg_ref[...]).astype(BF16)


def _window_specs(seq, ts, reverse):
    nt = seq // ts
    rb = ts // SUBLANES
    sb = seq // SUBLANES

    def tidx(j):
        return nt - 1 - j if reverse else j

    return [
        pl.BlockSpec((ts, LRU_WIDTH), lambda b, j: (b * nt + tidx(j), 0)),
        pl.BlockSpec((ts, LRU_WIDTH), lambda b, j: (b * nt + tidx(j), 0)),
        pl.BlockSpec((SUBLANES, LRU_WIDTH),
                     lambda b, j: (b * sb + jnp.maximum(tidx(j) * rb - 1, 0), 0)),
        pl.BlockSpec((SUBLANES, LRU_WIDTH),
                     lambda b, j: (b * sb + jnp.minimum((tidx(j) + 1) * rb, sb - 1), 0)),
    ]


def _mix(xl, xc, gg, u, vn, seq, lru, h0_f, h0_b, sgu_w, sgu_bias, *, ts, tc):
    conv_w, wg, b_r, b_i, lam = lru
    n = xl.shape[0]
    bsz = n // seq
    nt = seq // ts
    groups = ts // SUBLANES
    state_spec = pl.BlockSpec((1, 1, LRU_WIDTH), lambda b, j: (b, 0, 0))
    lru_specs = [
        _resident((4, LRU_WIDTH)),
        _resident((2, N_HALVES, HALF, 2 * HALF)),
        _resident((2, LRU_WIDTH)),
        _resident((2, LRU_WIDTH)),
        _resident((2, LRU_WIDTH)),
    ]
    lru_args = (conv_w, wg, b_r, b_i, lam)
    scan_scratch = [pltpu.VMEM((ts, LRU_WIDTH), F32)]
    carry_scratch = pltpu.VMEM((SUBLANES, LRU_WIDTH), F32)

    def tile_spec(width):
        return pl.BlockSpec((ts, width), lambda b, j: (b * nt + j, 0))

    hb = pl.pallas_call(
        functools.partial(_lru_bwd_kernel, ts=ts, nt=nt, tc=tc),
        grid=(bsz, nt),
        in_specs=_window_specs(seq, ts, True) + lru_specs + [state_spec],
        out_specs=pl.BlockSpec((groups, SUBLANES, LRU_WIDTH),
                               lambda b, j: (b * nt + nt - 1 - j, 0, 0)),
        out_shape=jax.ShapeDtypeStruct((n // SUBLANES, SUBLANES, LRU_WIDTH), F32),
        scratch_shapes=scan_scratch + [carry_scratch],
        compiler_params=_cparams(2),
        name="lru_bwd",
    )(xc, xl, xl, xl, *lru_args, h0_b)
    hb = hb.reshape(n, LRU_WIDTH)

    y, h_last = pl.pallas_call(
        functools.partial(_mix_fwd_kernel, ts=ts, nt=nt, tc=tc),
        grid=(bsz, nt),
        in_specs=_window_specs(seq, ts, False) + [
            tile_spec(LRU_WIDTH), tile_spec(MLP_WIDTH), tile_spec(MLP_WIDTH), tile_spec(LRU_WIDTH),
        ] + lru_specs + [
            state_spec,
            _resident((MLP_GROUPS, CHUNK, CHUNK)),
            _resident((CHUNK, MLP_WIDTH)),
        ],
        out_specs=[
            tile_spec(D_MODEL),
            pl.BlockSpec((1, SUBLANES, LRU_WIDTH), lambda b, j: (b, 0, 0)),
        ],
        out_shape=[
            jax.ShapeDtypeStruct((n, D_MODEL), BF16),
            jax.ShapeDtypeStruct((bsz, SUBLANES, LRU_WIDTH), F32),
        ],
        scratch_shapes=scan_scratch + [
            pltpu.VMEM((groups, SUBLANES, LRU_WIDTH), F32),
            carry_scratch,
        ],
        compiler_params=_cparams(2),
        name="mix_fwd",
    )(xc, xl, xl, xl, gg, u, vn, hb, *lru_args, h0_f, sgu_w, sgu_bias)
    return y, h_last[:, 0:1, :], hb


def _gate_weights(w_r, w_i):
    hpb = HALF // LRU_HEAD_DIM
    out = []
    for d in range(2):
        halves = []
        for hh in range(N_HALVES):
            blocks_r = [w_r[d, hh * hpb + k] for k in range(hpb)]
            blocks_i = [w_i[d, hh * hpb + k] for k in range(hpb)]
            halves.append(jnp.concatenate(
                [jax.scipy.linalg.block_diag(*blocks_r), jax.scipy.linalg.block_diag(*blocks_i)],
                axis=1))
        out.append(jnp.stack(halves))
    return (0.5 * jnp.stack(out)).astype(BF16)


def kernel(x, c, ctx, c_ctx, w_ada, b_ada, ffn1_norm_g, ffn1_w_in, ffn1_w_out,
           mix_norm_g, w_in_mix, lru_conv_w, lru_conv_b, lru_w_r, lru_b_r, lru_w_i,
           lru_b_i, lru_lambda, sgu_norm_g, sgu_w, sgu_b, w_out_mix,
           ffn2_norm_g, ffn2_w_in, ffn2_w_out, final_norm_g):
    bsz, n_lat, d = x.shape
    n_ctx = ctx.shape[1]
    depth = w_ada.shape[0]
    assert bsz + 1 <= MOD_ROWS and d == D_MODEL
    tm_lat, tm_ctx = 512, n_ctx

    c_rows = jnp.zeros((MOD_ROWS, d), F32).at[:bsz].set(c).at[bsz].set(c_ctx)
    mods = _ada(c_rows, w_ada, b_ada).reshape(depth, MOD_ROWS, N_MOD, d)

    h = x.reshape(bsz * n_lat, d)
    hc = ctx.reshape(bsz * n_ctx, d)
    zeros_state = jnp.zeros((bsz, 1, LRU_WIDTH), F32)
    w1_in, w1_out = _to_bf16(ffn1_w_in), _to_bf16(ffn1_w_out)
    w2_in, w2_out = _to_bf16(ffn2_w_in), _to_bf16(ffn2_w_out)
    w_mix_in, w_mix_out = _to_bf16(w_in_mix), _to_bf16(w_out_mix)

    for l in range(depth):
        last = l == depth - 1
        m = mods[l, :bsz]
        mc = mods[l, bsz:bsz + 1]
        lru = (lru_conv_w[l], _gate_weights(lru_w_r[l], lru_w_i[l]),
               lru_b_r[l], lru_b_i[l], lru_lambda[l])
        s_w = sgu_w[l].astype(BF16)
        s_bias = jnp.repeat(sgu_b[l].T, MLP_GROUP_DIM, axis=1)
        inproj = (mix_norm_g[l], w_mix_in, sgu_norm_g[l], lru_conv_w[l], lru_conv_b[l])

        hc, xl, xc, gg, u, vn = _ffn(hc, mc, 0, ffn1_norm_g[l], w1_in, w1_out, l, tm=tm_ctx,
                                 inproj=inproj)
        yc, hf_last, hb_c = _mix(xl, xc, gg, u, vn, n_ctx, lru, zeros_state, zeros_state,
                                 s_w, s_bias, ts=n_ctx, tc=tm_ctx)
        h0_b = hb_c.reshape(bsz, n_ctx, LRU_WIDTH)[:, 0:1, :]
        if not last:
            hc = _ffn(hc, mc, 6, ffn2_norm_g[l], w2_in, w2_out, l, tm=2 * tm_ctx,
                      outproj=(yc, w_mix_out))

        h, xl, xc, gg, u, vn = _ffn(h, m, 0, ffn1_norm_g[l], w1_in, w1_out, l, tm=tm_lat,
                                inproj=inproj)
        y, _, _ = _mix(xl, xc, gg, u, vn, n_lat, lru, hf_last, h0_b, s_w, s_bias,
                       ts=2 * tm_lat, tc=tm_lat)
        h = _ffn(h, m, 6, ffn2_norm_g[l], w2_in, w2_out, l, tm=2 * tm_lat,
                 outproj=(y, w_mix_out), final_g=final_norm_g if last else None)

    return h.reshape(bsz, n_lat, d)
```

```python
import functools

import jax
import jax.numpy as jnp
from jax import lax
from jax.experimental import pallas as pl
from jax.experimental.pallas import tpu as pltpu

F32 = jnp.float32
BF16 = jnp.bfloat16

EPS = 1e-6
D_MODEL = 1024
D_FF = 2816
LRU_WIDTH = 512
LRU_HEAD_DIM = 64
RG_C = 8.0
MLP_GROUPS = 4
MLP_GROUP_DIM = 128
MLP_WIDTH = 512
CHUNK = 128
IN_PROJ_WIDTH = 2048
N_MOD = 9
MOD_ROWS = 8
SUBLANES = 8
HALF = 256
N_HALVES = LRU_WIDTH // HALF
TINY = 1e-30
FFN_CHUNK = 256
CAST_BLOCK_ELEMS = 2 * 1024 * 1024

VMEM_LIMIT = 56 * 1024 * 1024


def _cparams(n_axes):
    return pltpu.CompilerParams(
        dimension_semantics=("arbitrary",) * n_axes,
        vmem_limit_bytes=VMEM_LIMIT,
    )


def _resident(shape, layer=None):
    nd = len(shape)
    if layer is None:
        return pl.BlockSpec(shape, lambda *_: (0,) * nd, pipeline_mode=pl.Buffered(1))
    return pl.BlockSpec((None,) + tuple(shape), lambda *_: (layer,) + (0,) * nd,
                        pipeline_mode=pl.Buffered(1))


def _cast_kernel(x_ref, o_ref):
    o_ref[...] = x_ref[...].astype(BF16)


def _to_bf16(w):
    depth, rows, cols = w.shape
    br = max(b for b in range(16, rows + 1, 16)
             if rows % b == 0 and b * cols <= CAST_BLOCK_ELEMS)
    spec = pl.BlockSpec((1, br, cols), lambda l, i: (l, i, 0))
    return pl.pallas_call(
        _cast_kernel,
        grid=(depth, rows // br),
        in_specs=[spec],
        out_specs=spec,
        out_shape=jax.ShapeDtypeStruct(w.shape, BF16),
        compiler_params=_cparams(2),
        name="to_bf16",
    )(w)


def _sigmoid(x):
    return 1.0 / (1.0 + jnp.exp(-x))


def _softplus(x):
    return jnp.maximum(x, 0.0) + jnp.log1p(jnp.exp(-jnp.abs(x)))


def _gelu_tanh(x):
    return 0.5 * x * (1.0 + jnp.tanh(0.7978845608028654 * (x + 0.044715 * (x * x * x))))


def _norm_mod(x, g, shift, scale):
    ms = jnp.mean(x * x, axis=-1, keepdims=True)
    return (x * lax.rsqrt(ms + EPS)) * (g * (1.0 + scale)) + shift


def _ada_kernel(c_ref, w_ref, b_ref, o_ref):
    c = c_ref[...]
    sc = c * _sigmoid(c)
    w = w_ref[0]
    s_hi = sc.astype(BF16)
    s_lo = (sc - s_hi.astype(F32)).astype(BF16)
    w_hi = w.astype(BF16)
    w_lo = (w - w_hi.astype(F32)).astype(BF16)
    p = jnp.dot(jnp.concatenate([s_hi, s_lo], axis=0), w_hi, preferred_element_type=F32)
    p = p[:MOD_ROWS] + p[MOD_ROWS:] + jnp.dot(s_hi, w_lo, preferred_element_type=F32)
    o_ref[0] = p + b_ref[0]


def _ada(c_rows, w_ada, b_ada):
    depth = w_ada.shape[0]
    return pl.pallas_call(
        _ada_kernel,
        grid=(depth, N_MOD),
        in_specs=[
            pl.BlockSpec((MOD_ROWS, D_MODEL), lambda l, j: (0, 0)),
            pl.BlockSpec((1, D_MODEL, D_MODEL), lambda l, j: (l, 0, j)),
            pl.BlockSpec((1, 1, D_MODEL), lambda l, j: (l, 0, j)),
        ],
        out_specs=pl.BlockSpec((1, MOD_ROWS, D_MODEL), lambda l, j: (l, 0, j)),
        out_shape=jax.ShapeDtypeStruct((depth, MOD_ROWS, N_MOD * D_MODEL), F32),
        compiler_params=_cparams(2),
        name="ada_mod",
    )(c_rows, w_ada, b_ada.reshape(depth, 1, N_MOD * D_MODEL))


def _ffn_kernel(*refs, k0, pro, epi):
    refs = list(refs)
    x_ref, mod_ref, g_ref, win_ref, wout_ref = refs[:5]
    pos = 5
    if pro == "outproj":
        y_ref, wo_ref = refs[pos:pos + 2]
        pos += 2
    if epi == "inproj":
        gm_ref, wmix_ref, sg_ref, cw_ref, cb_ref = refs[pos:pos + 5]
        pos += 5
    elif epi == "final":
        fg_ref = refs[pos]
        pos += 1
    o_ref = refs[pos]
    pos += 1
    if epi == "inproj":
        xl_ref, xc_ref, gg_ref, u_ref, vn_ref = refs[pos:pos + 5]
        pos += 5
    act_ref = refs[pos]
    pos += 1
    if epi == "inproj":
        xe_ref = refs[pos]
        tm = x_ref.shape[0]

        @pl.when(pl.program_id(0) == 0)
        def _():
            xe_ref[...] = jnp.zeros_like(xe_ref)

    h = x_ref[...]
    if pro == "outproj":
        h = h + mod_ref[0, 5:6, :] * jnp.dot(y_ref[...], wo_ref[...], preferred_element_type=F32)
    shift = mod_ref[0, k0:k0 + 1, :]
    scale = mod_ref[0, k0 + 1:k0 + 2, :]
    gate = mod_ref[0, k0 + 2:k0 + 3, :]
    zb = _norm_mod(h, g_ref[...], shift, scale).astype(BF16)
    for c0 in range(0, D_FF, FFN_CHUNK):
        cw = min(FFN_CHUNK, D_FF - c0)
        g = jnp.dot(zb, win_ref[:, c0:c0 + cw], preferred_element_type=F32)
        u = jnp.dot(zb, win_ref[:, D_FF + c0:D_FF + c0 + cw], preferred_element_type=F32)
        act_ref[:, c0:c0 + cw] = (g * _sigmoid(g) * u).astype(BF16)
    out = h + (0.5 * gate) * jnp.dot(act_ref[...], wout_ref[...], preferred_element_type=F32)
    if epi == "final":
        ms = jnp.mean(out * out, axis=-1, keepdims=True)
        out = out * lax.rsqrt(ms + EPS) * fg_ref[...]
    o_ref[...] = out
    if epi == "inproj":
        zm = _norm_mod(out, gm_ref[...], mod_ref[0, 3:4, :], mod_ref[0, 4:5, :]).astype(BF16)
        v = jnp.dot(zm, wmix_ref[:, 2 * LRU_WIDTH + MLP_WIDTH:], preferred_element_type=F32)
        vc = v - jnp.mean(v, axis=-1, keepdims=True)
        var = jnp.mean(vc * vc, axis=-1, keepdims=True)
        vn_ref[...] = (vc * lax.rsqrt(var + EPS) * sg_ref[...]).astype(BF16)
        xl = jnp.dot(zm, wmix_ref[:, 0:LRU_WIDTH], preferred_element_type=F32)
        xl_ref[...] = xl
        xe_ref[SUBLANES:SUBLANES + tm] = xl
        xc_ref[...] = _short_conv(xe_ref, cw_ref, cb_ref, tm, SUBLANES)
        gl = jnp.dot(zm, wmix_ref[:, LRU_WIDTH:2 * LRU_WIDTH], preferred_element_type=F32)
        gg_ref[...] = _gelu_tanh(gl)
        u_ref[...] = jnp.dot(zm, wmix_ref[:, 2 * LRU_WIDTH:2 * LRU_WIDTH + MLP_WIDTH],
                             preferred_element_type=F32)


def _ffn(x, mods, mod_rows, k0, norm_g, w_in, w_out, layer, *, tm, outproj=None, inproj=None,
         final_g=None):
    n = x.shape[0]
    mod_row0, n_mod_rows = mod_rows
    tiles_per_mod = n // n_mod_rows // tm
    pro = "outproj" if outproj is not None else None
    epi = "inproj" if inproj is not None else ("final" if final_g is not None else None)

    def row_spec(width):
        return pl.BlockSpec((tm, width), lambda i: (i, 0))

    args = [x, mods, norm_g.reshape(1, D_MODEL), w_in, w_out]
    in_specs = [
        row_spec(D_MODEL),
        pl.BlockSpec((None, 1, N_MOD, D_MODEL),
                     lambda i: (layer, mod_row0 + i // tiles_per_mod, 0, 0)),
        _resident((1, D_MODEL)),
        _resident((D_MODEL, 2 * D_FF), layer),
        _resident((D_FF, D_MODEL), layer),
    ]
    out_shape = [jax.ShapeDtypeStruct((n, D_MODEL), F32)]
    out_specs = [row_spec(D_MODEL)]
    if pro == "outproj":
        args += [outproj[0], outproj[1]]
        in_specs += [row_spec(D_MODEL), _resident((D_MODEL, D_MODEL), layer)]
    scratch = [pltpu.VMEM((tm, D_FF), BF16)]
    if epi == "inproj":
        args += [inproj[0].reshape(1, D_MODEL), inproj[1], inproj[2].reshape(1, MLP_WIDTH),
                 inproj[3], inproj[4].reshape(1, LRU_WIDTH)]
        in_specs += [_resident((1, D_MODEL)), _resident((D_MODEL, IN_PROJ_WIDTH), layer),
                     _resident((1, MLP_WIDTH)), _resident((4, LRU_WIDTH)),
                     _resident((1, LRU_WIDTH))]
        out_shape += [jax.ShapeDtypeStruct((n, LRU_WIDTH), F32),
                      jax.ShapeDtypeStruct((n, LRU_WIDTH), F32),
                      jax.ShapeDtypeStruct((n, LRU_WIDTH), F32),
                      jax.ShapeDtypeStruct((n, MLP_WIDTH), F32),
                      jax.ShapeDtypeStruct((n, MLP_WIDTH), BF16)]
        out_specs += [row_spec(LRU_WIDTH), row_spec(LRU_WIDTH), row_spec(LRU_WIDTH),
                      row_spec(MLP_WIDTH), row_spec(MLP_WIDTH)]
        scratch += [pltpu.VMEM((tm + 2 * SUBLANES, LRU_WIDTH), F32)]
    elif epi == "final":
        args += [final_g.reshape(1, D_MODEL)]
        in_specs += [_resident((1, D_MODEL))]
    res = pl.pallas_call(
        functools.partial(_ffn_kernel, k0=k0, pro=pro, epi=epi),
        grid=(n // tm,),
        in_specs=in_specs,
        out_specs=out_specs,
        out_shape=out_shape,
        scratch_shapes=scratch,
        compiler_params=_cparams(1),
        name="ffn_" + (pro or "plain") + "_" + (epi or "plain"),
    )(*args)
    return res if epi == "inproj" else res[0]


def _short_conv(xe_ref, cw_ref, cb_ref, ts, halo):
    xe = xe_ref[...]
    rows = ts + 2 * halo
    xc = cb_ref[...] + xe[halo:halo + ts] * cw_ref[2:3, :]
    for k, sh in ((0, 2), (1, 1), (3, rows - 1)):
        xc = xc + pltpu.roll(xe, sh, 0)[halo:halo + ts] * cw_ref[k:k + 1, :]
    return xc


def _finish_conv(xc_ref, xl_ref, xp_ref, xn_ref, cw_ref, xcs_ref, tile, nt, ts, tc):
    zeros = jnp.zeros((SUBLANES, LRU_WIDTH), F32)
    xcs_ref[...] = xc_ref[...]
    for r0 in range(0, ts, tc):
        r1 = r0 + tc
        before = jnp.where(tile > 0, xp_ref[...], 0.0) if r0 == 0 else xl_ref[r0 - SUBLANES:r0]
        after = jnp.where(tile < nt - 1, xn_ref[...], 0.0) if r1 == ts else xl_ref[r1:r1 + SUBLANES]
        e = jnp.concatenate([before, zeros], axis=0)
        head = (pltpu.roll(e, 2, 0)[SUBLANES:] * cw_ref[0:1, :]
                + pltpu.roll(e, 1, 0)[SUBLANES:] * cw_ref[1:2, :])
        f = jnp.concatenate([zeros, after], axis=0)
        tail = pltpu.roll(f, 2 * SUBLANES - 1, 0)[0:SUBLANES] * cw_ref[3:4, :]
        xcs_ref[r0:r0 + SUBLANES] = xcs_ref[r0:r0 + SUBLANES] + head
        xcs_ref[r1 - SUBLANES:r1] = xcs_ref[r1 - SUBLANES:r1] + tail


def _lru_scan(d, xc, wg_ref, br_ref, bi_ref, lam_ref, h_ref, g0, carry, ts):
    groups = ts // SUBLANES
    sub = lax.broadcasted_iota(jnp.int32, (groups, SUBLANES, HALF), 1)
    order = range(groups) if d == 0 else range(groups - 1, -1, -1)
    row_id = lax.broadcasted_iota(jnp.int32, (SUBLANES, HALF), 0)
    near = row_id < SUBLANES // 2 if d == 0 else row_id >= SUBLANES // 2
    carry_out = []
    for hh in range(N_HALVES):
        cols = slice(hh * HALF, (hh + 1) * HALF)
        xch = xc[:, cols]
        kk = (-0.5 * RG_C) * _softplus(-lam_ref[d:d + 1, cols])
        pre = jnp.dot(xch.astype(BF16), wg_ref[d, hh], preferred_element_type=F32)
        tr = jnp.tanh(pre[:, :HALF] + 0.5 * br_ref[d:d + 1, cols])
        ti = jnp.tanh(pre[:, HALF:] + 0.5 * bi_ref[d:d + 1, cols])
        log_a = kk + kk * tr
        a = jnp.exp(log_a)
        t = jnp.tanh(log_a)
        q = t / (t - 1.0)
        gain = q * lax.rsqrt(jnp.maximum(q, TINY))
        bc = gain * ((0.7071067811865476 * xch) * (1.0 + ti))
        a3 = a.reshape(groups, SUBLANES, HALF)
        b3 = bc.reshape(groups, SUBLANES, HALF)
        for s in (1, 2):
            if d == 0:
                sh, m = s, sub >= s
            else:
                sh, m = SUBLANES - s, sub < SUBLANES - s
            am = jnp.where(m, a3, 0.0)
            b3 = b3 + am * pltpu.roll(b3, sh, 1)
            a3 = jnp.where(m, a3 * pltpu.roll(a3, sh, 1), a3)
        c = carry[:, cols]
        for g in order:
            h_near = a3[g] * c + b3[g]
            h_far = a3[g] * pltpu.roll(h_near, SUBLANES // 2, 0) + b3[g]
            h = jnp.where(near, h_near, h_far)
            h_ref[g0 + g, :, cols] = h
            row = h[SUBLANES - 1:SUBLANES, :] if d == 0 else h[0:1, :]
            c = jnp.broadcast_to(row, (SUBLANES, HALF))
        carry_out.append(c)
    return jnp.concatenate(carry_out, axis=1)


def _lru_bwd_kernel(xc_ref, xl_ref, xp_ref, xn_ref, cw_ref, wg_ref, br_ref, bi_ref, lam_ref,
                    h0_ref, hb_ref, xcs_ref, carry_ref, *, ts, nt, tc):
    j = pl.program_id(1)

    @pl.when(j == 0)
    def _():
        carry_ref[...] = jnp.broadcast_to(h0_ref[0], (SUBLANES, LRU_WIDTH))

    _finish_conv(xc_ref, xl_ref, xp_ref, xn_ref, cw_ref, xcs_ref, nt - 1 - j, nt, ts, tc)
    carry = carry_ref[...]
    for r0 in range(ts - tc, -1, -tc):
        carry = _lru_scan(1, xcs_ref[r0:r0 + tc], wg_ref, br_ref, bi_ref, lam_ref, hb_ref,
                          r0 // SUBLANES, carry, tc)
    carry_ref[...] = carry


def _mix_fwd_kernel(xc_ref, xl_ref, xp_ref, xn_ref, gg_ref, u_ref, vn_ref, hb_ref,
                    cw_ref, wg_ref, br_ref, bi_ref, lam_ref, h0_ref, sw_ref, sb_ref,
                    y_ref, hlast_ref, xcs_ref, hf_ref, carry_ref, *, ts, nt, tc):
    j = pl.program_id(1)
    n_chunks = ts // CHUNK

    @pl.when(j == 0)
    def _():
        carry_ref[...] = jnp.broadcast_to(h0_ref[0], (SUBLANES, LRU_WIDTH))

    _finish_conv(xc_ref, xl_ref, xp_ref, xn_ref, cw_ref, xcs_ref, j, nt, ts, tc)
    carry = carry_ref[...]
    for r0 in range(0, ts, tc):
        carry = _lru_scan(0, xcs_ref[r0:r0 + tc], wg_ref, br_ref, bi_ref, lam_ref, hf_ref,
                          r0 // SUBLANES, carry, tc)
    carry_ref[...] = carry
    hlast_ref[0] = carry
    for g in range(MLP_GROUPS):
        cols = slice(g * MLP_GROUP_DIM, (g + 1) * MLP_GROUP_DIM)
        rhs = jnp.concatenate(
            [vn_ref[ch * CHUNK:(ch + 1) * CHUNK, cols] for ch in range(n_chunks)], axis=1)
        z = jnp.dot(sw_ref[g], rhs, preferred_element_type=F32)
        for ch in range(n_chunks):
            rows = slice(ch * CHUNK, (ch + 1) * CHUNK)
            zc = z[:, ch * MLP_GROUP_DIM:(ch + 1) * MLP_GROUP_DIM] + sb_ref[:, cols]
            y_ref[rows, LRU_WIDTH + g * MLP_GROUP_DIM:LRU_WIDTH + (g + 1) * MLP_GROUP_DIM] = (
                u_ref[rows, cols] * zc).astype(BF16)

    h_sum = hf_ref[...].reshape(ts, LRU_WIDTH) + hb_ref[...]
    y_ref[:, 0:LRU_WIDTH] = (h_sum * gg_ref[...]).astype(BF16)


def _window_specs(seq, ts, reverse):
    nt = seq // ts
    rb = ts // SUBLANES
    sb = seq // SUBLANES

    def tidx(j):
        return nt - 1 - j if reverse else j

    return [
        pl.BlockSpec((ts, LRU_WIDTH), lambda b, j: (b * nt + tidx(j), 0)),
        pl.BlockSpec((ts, LRU_WIDTH), lambda b, j: (b * nt + tidx(j), 0)),
        pl.BlockSpec((SUBLANES, LRU_WIDTH),
                     lambda b, j: (b * sb + jnp.maximum(tidx(j) * rb - 1, 0), 0)),
        pl.BlockSpec((SUBLANES, LRU_WIDTH),
                     lambda b, j: (b * sb + jnp.minimum((tidx(j) + 1) * rb, sb - 1), 0)),
    ]


def _mix(xl, xc, gg, u, vn, seq, lru, h0_f, h0_b, sgu_w, sgu_bias, *, ts, tc):
    conv_w, wg, b_r, b_i, lam = lru
    n = xl.shape[0]
    bsz = n // seq
    nt = seq // ts
    groups = ts // SUBLANES
    state_spec = pl.BlockSpec((1, 1, LRU_WIDTH), lambda b, j: (b, 0, 0))
    lru_specs = [
        _resident((4, LRU_WIDTH)),
        _resident((2, N_HALVES, HALF, 2 * HALF)),
        _resident((2, LRU_WIDTH)),
        _resident((2, LRU_WIDTH)),
        _resident((2, LRU_WIDTH)),
    ]
    lru_args = (conv_w, wg, b_r, b_i, lam)
    scan_scratch = [pltpu.VMEM((ts, LRU_WIDTH), F32)]
    carry_scratch = pltpu.VMEM((SUBLANES, LRU_WIDTH), F32)

    def tile_spec(width):
        return pl.BlockSpec((ts, width), lambda b, j: (b * nt + j, 0))

    hb = pl.pallas_call(
        functools.partial(_lru_bwd_kernel, ts=ts, nt=nt, tc=tc),
        grid=(bsz, nt),
        in_specs=_window_specs(seq, ts, True) + lru_specs + [state_spec],
        out_specs=pl.BlockSpec((groups, SUBLANES, LRU_WIDTH),
                               lambda b, j: (b * nt + nt - 1 - j, 0, 0)),
        out_shape=jax.ShapeDtypeStruct((n // SUBLANES, SUBLANES, LRU_WIDTH), F32),
        scratch_shapes=scan_scratch + [carry_scratch],
        compiler_params=_cparams(2),
        name="lru_bwd",
    )(xc, xl, xl, xl, *lru_args, h0_b)
    hb = hb.reshape(n, LRU_WIDTH)

    y, h_last = pl.pallas_call(
        functools.partial(_mix_fwd_kernel, ts=ts, nt=nt, tc=tc),
        grid=(bsz, nt),
        in_specs=_window_specs(seq, ts, False) + [
            tile_spec(LRU_WIDTH), tile_spec(MLP_WIDTH), tile_spec(MLP_WIDTH), tile_spec(LRU_WIDTH),
        ] + lru_specs + [
            state_spec,
            _resident((MLP_GROUPS, CHUNK, CHUNK)),
            _resident((CHUNK, MLP_WIDTH)),
        ],
        out_specs=[
            tile_spec(D_MODEL),
            pl.BlockSpec((1, SUBLANES, LRU_WIDTH), lambda b, j: (b, 0, 0)),
        ],
        out_shape=[
            jax.ShapeDtypeStruct((n, D_MODEL), BF16),
            jax.ShapeDtypeStruct((bsz, SUBLANES, LRU_WIDTH), F32),
        ],
        scratch_shapes=scan_scratch + [
            pltpu.VMEM((groups, SUBLANES, LRU_WIDTH), F32),
            carry_scratch,
        ],
        compiler_params=_cparams(2),
        name="mix_fwd",
    )(xc, xl, xl, xl, gg, u, vn, hb, *lru_args, h0_f, sgu_w, sgu_bias)
    return y, h_last[:, 0:1, :], hb


def _gate_weights(w_r, w_i):
    hpb = HALF // LRU_HEAD_DIM
    eye = jnp.eye(hpb, dtype=F32)[None, None, :, None, :, None]

    def block_diag(w):
        w6 = w.reshape(2, N_HALVES, hpb, LRU_HEAD_DIM, 1, LRU_HEAD_DIM)
        return (w6 * eye).reshape(2, N_HALVES, HALF, HALF)

    return (0.5 * jnp.concatenate([block_diag(w_r), block_diag(w_i)], axis=-1)).astype(BF16)


def kernel(x, c, ctx, c_ctx, w_ada, b_ada, ffn1_norm_g, ffn1_w_in, ffn1_w_out,
           mix_norm_g, w_in_mix, lru_conv_w, lru_conv_b, lru_w_r, lru_b_r, lru_w_i,
           lru_b_i, lru_lambda, sgu_norm_g, sgu_w, sgu_b, w_out_mix,
           ffn2_norm_g, ffn2_w_in, ffn2_w_out, final_norm_g):
    bsz, n_lat, d = x.shape
    n_ctx = ctx.shape[1]
    depth = w_ada.shape[0]
    assert bsz + 1 <= MOD_ROWS and d == D_MODEL
    tm_lat, tm_ctx = 512, n_ctx

    c_rows = jnp.zeros((MOD_ROWS, d), F32).at[:bsz].set(c).at[bsz].set(c_ctx)
    mods = _ada(c_rows, w_ada, b_ada).reshape(depth, MOD_ROWS, N_MOD, d)

    h = x.reshape(bsz * n_lat, d)
    hc = ctx.reshape(bsz * n_ctx, d)
    zeros_state = jnp.zeros((bsz, 1, LRU_WIDTH), F32)
    m, mc = (0, bsz), (bsz, 1)
    w1_in, w1_out = _to_bf16(ffn1_w_in), _to_bf16(ffn1_w_out)
    w2_in, w2_out = _to_bf16(ffn2_w_in), _to_bf16(ffn2_w_out)
    w_mix_in, w_mix_out = _to_bf16(w_in_mix), _to_bf16(w_out_mix)

    for l in range(depth):
        last = l == depth - 1
        lru = (lru_conv_w[l], _gate_weights(lru_w_r[l], lru_w_i[l]),
               lru_b_r[l], lru_b_i[l], lru_lambda[l])
        s_w = sgu_w[l].astype(BF16)
        s_bias = jnp.repeat(sgu_b[l].T, MLP_GROUP_DIM, axis=1)
        inproj = (mix_norm_g[l], w_mix_in, sgu_norm_g[l], lru_conv_w[l], lru_conv_b[l])

        hc, xl, xc, gg, u, vn = _ffn(hc, mods, mc, 0, ffn1_norm_g[l], w1_in, w1_out, l,
                                 tm=tm_ctx, inproj=inproj)
        yc, hf_last, hb_c = _mix(xl, xc, gg, u, vn, n_ctx, lru, zeros_state, zeros_state,
                                 s_w, s_bias, ts=n_ctx, tc=tm_ctx)
        h0_b = hb_c.reshape(bsz, n_ctx, LRU_WIDTH)[:, 0:1, :]
        if not last:
            hc = _ffn(hc, mods, mc, 6, ffn2_norm_g[l], w2_in, w2_out, l, tm=2 * tm_ctx,
                      outproj=(yc, w_mix_out))

        h, xl, xc, gg, u, vn = _ffn(h, mods, m, 0, ffn1_norm_g[l], w1_in, w1_out, l,
                                tm=tm_lat, inproj=inproj)
        y, _, _ = _mix(xl, xc, gg, u, vn, n_lat, lru, hf_last, h0_b, s_w, s_bias,
                       ts=2 * tm_lat, tc=tm_lat)
        h = _ffn(h, mods, m, 6, ffn2_norm_g[l], w2_in, w2_out, l, tm=2 * tm_lat,
                 outproj=(y, w_mix_out), final_g=final_norm_g if last else None)

    return h.reshape(bsz, n_lat, d)
```

```python
import functools

import jax
import jax.numpy as jnp
from jax import lax
from jax.experimental import pallas as pl
from jax.experimental.pallas import tpu as pltpu

F32 = jnp.float32
BF16 = jnp.bfloat16

EPS = 1e-6
D_MODEL = 1024
D_FF = 2816
LRU_WIDTH = 512
LRU_HEADS = 8
LRU_HEAD_DIM = 64
RG_C = 8.0
MLP_GROUPS = 4
MLP_GROUP_DIM = 128
MLP_WIDTH = 512
CHUNK = 128
IN_PROJ_WIDTH = 2048
N_MOD = 9
MOD_ROWS = 8
SUBLANES = 8
HALF = 256
N_HALVES = LRU_WIDTH // HALF
TINY = 1e-30
FFN_CHUNK = 256
CAST_BLOCK_ELEMS = 2 * 1024 * 1024

VMEM_LIMIT = 56 * 1024 * 1024


def _cparams(n_axes):
    return pltpu.CompilerParams(
        dimension_semantics=("arbitrary",) * n_axes,
        vmem_limit_bytes=VMEM_LIMIT,
    )


def _resident(shape, layer=None):
    nd = len(shape)
    if layer is None:
        return pl.BlockSpec(shape, lambda *_: (0,) * nd, pipeline_mode=pl.Buffered(1))
    return pl.BlockSpec((None,) + tuple(shape), lambda *_: (layer,) + (0,) * nd,
                        pipeline_mode=pl.Buffered(1))


def _cast_kernel(x_ref, o_ref):
    o_ref[...] = x_ref[...].astype(BF16)


def _to_bf16(w):
    depth, rows, cols = w.shape
    br = max(b for b in range(16, rows + 1, 16)
             if rows % b == 0 and b * cols <= CAST_BLOCK_ELEMS)
    spec = pl.BlockSpec((1, br, cols), lambda l, i: (l, i, 0))
    return pl.pallas_call(
        _cast_kernel,
        grid=(depth, rows // br),
        in_specs=[spec],
        out_specs=spec,
        out_shape=jax.ShapeDtypeStruct(w.shape, BF16),
        compiler_params=_cparams(2),
        name="to_bf16",
    )(w)


def _sigmoid(x):
    return 1.0 / (1.0 + jnp.exp(-x))


def _softplus(x):
    return jnp.maximum(x, 0.0) + jnp.log1p(jnp.exp(-jnp.abs(x)))


def _gelu_tanh(x):
    return 0.5 * x * (1.0 + jnp.tanh(0.7978845608028654 * (x + 0.044715 * (x * x * x))))


def _norm_mod(x, g, shift, scale):
    ms = jnp.mean(x * x, axis=-1, keepdims=True)
    return (x * lax.rsqrt(ms + EPS)) * (g * (1.0 + scale)) + shift


def _ada_kernel(c_ref, w_ref, b_ref, o_ref):
    c = c_ref[...]
    sc = c * _sigmoid(c)
    w = w_ref[0]
    s_hi = sc.astype(BF16)
    s_lo = (sc - s_hi.astype(F32)).astype(BF16)
    w_hi = w.astype(BF16)
    w_lo = (w - w_hi.astype(F32)).astype(BF16)
    p = jnp.dot(jnp.concatenate([s_hi, s_lo], axis=0), w_hi, preferred_element_type=F32)
    p = p[:MOD_ROWS] + p[MOD_ROWS:] + jnp.dot(s_hi, w_lo, preferred_element_type=F32)
    o_ref[0] = p + b_ref[0]


def _ada(c_rows, w_ada, b_ada):
    depth = w_ada.shape[0]
    return pl.pallas_call(
        _ada_kernel,
        grid=(depth, N_MOD),
        in_specs=[
            pl.BlockSpec((MOD_ROWS, D_MODEL), lambda l, j: (0, 0)),
            pl.BlockSpec((1, D_MODEL, D_MODEL), lambda l, j: (l, 0, j)),
            pl.BlockSpec((1, 1, D_MODEL), lambda l, j: (l, 0, j)),
        ],
        out_specs=pl.BlockSpec((1, MOD_ROWS, D_MODEL), lambda l, j: (l, 0, j)),
        out_shape=jax.ShapeDtypeStruct((depth, MOD_ROWS, N_MOD * D_MODEL), F32),
        compiler_params=_cparams(2),
        name="ada_mod",
    )(c_rows, w_ada, b_ada.reshape(depth, 1, N_MOD * D_MODEL))


def _ffn_kernel(*refs, k0, pro, epi):
    refs = list(refs)
    x_ref, mod_ref, g_ref, win_ref, wout_ref = refs[:5]
    pos = 5
    if pro == "outproj":
        y_ref, wo_ref = refs[pos:pos + 2]
        pos += 2
    if epi == "inproj":
        gm_ref, wmix_ref, sg_ref, cw_ref, cb_ref = refs[pos:pos + 5]
        pos += 5
    elif epi == "final":
        fg_ref = refs[pos]
        pos += 1
    o_ref = refs[pos]
    pos += 1
    if epi == "inproj":
        xl_ref, xc_ref, gg_ref, u_ref, vn_ref = refs[pos:pos + 5]
        pos += 5
    act_ref = refs[pos]
    pos += 1
    if epi == "inproj":
        xe_ref = refs[pos]
        tm = x_ref.shape[0]

        @pl.when(pl.program_id(0) == 0)
        def _():
            xe_ref[...] = jnp.zeros_like(xe_ref)

    shift = mod_ref[0, k0:k0 + 1, :]
    scale = mod_ref[0, k0 + 1:k0 + 2, :]
    gate = mod_ref[0, k0 + 2:k0 + 3, :]
    if pro == "outproj":
        hm = x_ref.shape[0] // 2
        hs, zs = [], []
        for r in (slice(0, hm), slice(hm, 2 * hm)):
            hr = x_ref[r, :] + mod_ref[0, 5:6, :] * jnp.dot(y_ref[r, :], wo_ref[...],
                                                            preferred_element_type=F32)
            hs.append(hr)
            zs.append(_norm_mod(hr, g_ref[...], shift, scale).astype(BF16))
        h = jnp.concatenate(hs, axis=0)
        zb = jnp.concatenate(zs, axis=0)
    else:
        h = x_ref[...]
        zb = _norm_mod(h, g_ref[...], shift, scale).astype(BF16)
    for c0 in range(0, D_FF, FFN_CHUNK):
        cw = min(FFN_CHUNK, D_FF - c0)
        g = jnp.dot(zb, win_ref[:, c0:c0 + cw], preferred_element_type=F32)
        u = jnp.dot(zb, win_ref[:, D_FF + c0:D_FF + c0 + cw], preferred_element_type=F32)
        act_ref[:, c0:c0 + cw] = (g * _sigmoid(g) * u).astype(BF16)
    out = h + (0.5 * gate) * jnp.dot(act_ref[...], wout_ref[...], preferred_element_type=F32)
    if epi == "final":
        ms = jnp.mean(out * out, axis=-1, keepdims=True)
        out = out * lax.rsqrt(ms + EPS) * fg_ref[...]
    o_ref[...] = out
    if epi == "inproj":
        zm = _norm_mod(out, gm_ref[...], mod_ref[0, 3:4, :], mod_ref[0, 4:5, :]).astype(BF16)
        v = jnp.dot(zm, wmix_ref[:, 2 * LRU_WIDTH + MLP_WIDTH:], preferred_element_type=F32)
        vc = v - jnp.mean(v, axis=-1, keepdims=True)
        var = jnp.mean(vc * vc, axis=-1, keepdims=True)
        vn_ref[...] = (vc * lax.rsqrt(var + EPS) * sg_ref[...]).astype(BF16)
        xl = jnp.dot(zm, wmix_ref[:, 0:LRU_WIDTH], preferred_element_type=F32)
        xl_ref[...] = xl
        xe_ref[SUBLANES:SUBLANES + tm] = xl
        xc_ref[...] = _short_conv(xe_ref, cw_ref, cb_ref, tm, SUBLANES)
        gl = jnp.dot(zm, wmix_ref[:, LRU_WIDTH:2 * LRU_WIDTH], preferred_element_type=F32)
        gg_ref[...] = _gelu_tanh(gl)
        u_ref[...] = jnp.dot(zm, wmix_ref[:, 2 * LRU_WIDTH:2 * LRU_WIDTH + MLP_WIDTH],
                             preferred_element_type=F32)


def _ffn(x, mod, k0, norm_g, w_in, w_out, layer, *, tm, outproj=None, inproj=None,
         final_g=None):
    n = x.shape[0]
    tiles_per_mod = n // mod.shape[0] // tm
    pro = "outproj" if outproj is not None else None
    epi = "inproj" if inproj is not None else ("final" if final_g is not None else None)

    def row_spec(width):
        return pl.BlockSpec((tm, width), lambda i: (i, 0))

    args = [x, mod, norm_g.reshape(1, D_MODEL), w_in, w_out]
    in_specs = [
        row_spec(D_MODEL),
        pl.BlockSpec((1, N_MOD, D_MODEL), lambda i: (i // tiles_per_mod, 0, 0)),
        _resident((1, D_MODEL)),
        _resident((D_MODEL, 2 * D_FF), layer),
        _resident((D_FF, D_MODEL), layer),
    ]
    out_shape = [jax.ShapeDtypeStruct((n, D_MODEL), F32)]
    out_specs = [row_spec(D_MODEL)]
    if pro == "outproj":
        args += [outproj[0], outproj[1]]
        in_specs += [row_spec(D_MODEL), _resident((D_MODEL, D_MODEL), layer)]
    scratch = [pltpu.VMEM((tm, D_FF), BF16)]
    if epi == "inproj":
        args += [inproj[0].reshape(1, D_MODEL), inproj[1], inproj[2].reshape(1, MLP_WIDTH),
                 inproj[3], inproj[4].reshape(1, LRU_WIDTH)]
        in_specs += [_resident((1, D_MODEL)), _resident((D_MODEL, IN_PROJ_WIDTH), layer),
                     _resident((1, MLP_WIDTH)), _resident((4, LRU_WIDTH)),
                     _resident((1, LRU_WIDTH))]
        out_shape += [jax.ShapeDtypeStruct((n, LRU_WIDTH), F32),
                      jax.ShapeDtypeStruct((n, LRU_WIDTH), F32),
                      jax.ShapeDtypeStruct((n, LRU_WIDTH), F32),
                      jax.ShapeDtypeStruct((n, MLP_WIDTH), F32),
                      jax.ShapeDtypeStruct((n, MLP_WIDTH), BF16)]
        out_specs += [row_spec(LRU_WIDTH), row_spec(LRU_WIDTH), row_spec(LRU_WIDTH),
                      row_spec(MLP_WIDTH), row_spec(MLP_WIDTH)]
        scratch += [pltpu.VMEM((tm + 2 * SUBLANES, LRU_WIDTH), F32)]
    elif epi == "final":
        args += [final_g.reshape(1, D_MODEL)]
        in_specs += [_resident((1, D_MODEL))]
    res = pl.pallas_call(
        functools.partial(_ffn_kernel, k0=k0, pro=pro, epi=epi),
        grid=(n // tm,),
        in_specs=in_specs,
        out_specs=out_specs,
        out_shape=out_shape,
        scratch_shapes=scratch,
        compiler_params=_cparams(1),
        name="ffn_" + (pro or "plain") + "_" + (epi or "plain"),
    )(*args)
    return res if epi == "inproj" else res[0]


def _short_conv(xe_ref, cw_ref, cb_ref, ts, halo):
    xe = xe_ref[...]
    rows = ts + 2 * halo
    xc = cb_ref[...] + xe[halo:halo + ts] * cw_ref[2:3, :]
    for k, sh in ((0, 2), (1, 1), (3, rows - 1)):
        xc = xc + pltpu.roll(xe, sh, 0)[halo:halo + ts] * cw_ref[k:k + 1, :]
    return xc


def _finish_conv(xc_ref, xl_ref, xp_ref, xn_ref, cw_ref, xcs_ref, tile, nt, ts, tc):
    zeros = jnp.zeros((SUBLANES, LRU_WIDTH), F32)
    xcs_ref[...] = xc_ref[...]
    for r0 in range(0, ts, tc):
        r1 = r0 + tc
        before = jnp.where(tile > 0, xp_ref[...], 0.0) if r0 == 0 else xl_ref[r0 - SUBLANES:r0]
        after = jnp.where(tile < nt - 1, xn_ref[...], 0.0) if r1 == ts else xl_ref[r1:r1 + SUBLANES]
        e = jnp.concatenate([before, zeros], axis=0)
        head = (pltpu.roll(e, 2, 0)[SUBLANES:] * cw_ref[0:1, :]
                + pltpu.roll(e, 1, 0)[SUBLANES:] * cw_ref[1:2, :])
        f = jnp.concatenate([zeros, after], axis=0)
        tail = pltpu.roll(f, 2 * SUBLANES - 1, 0)[0:SUBLANES] * cw_ref[3:4, :]
        xcs_ref[r0:r0 + SUBLANES] = xcs_ref[r0:r0 + SUBLANES] + head
        xcs_ref[r1 - SUBLANES:r1] = xcs_ref[r1 - SUBLANES:r1] + tail


def _lru_scan(d, xc, wg_ref, br_ref, bi_ref, lam_ref, h_ref, g0, carry, ts):
    groups = ts // SUBLANES
    sub = lax.broadcasted_iota(jnp.int32, (groups, SUBLANES, HALF), 1)
    order = range(groups) if d == 0 else range(groups - 1, -1, -1)
    row_id = lax.broadcasted_iota(jnp.int32, (SUBLANES, HALF), 0)
    near = row_id < SUBLANES // 2 if d == 0 else row_id >= SUBLANES // 2
    carry_out = []
    for hh in range(N_HALVES):
        cols = slice(hh * HALF, (hh + 1) * HALF)
        xch = xc[:, cols]
        kk = (-0.5 * RG_C) * _softplus(-lam_ref[d:d + 1, cols])
        pre = jnp.dot(xch.astype(BF16), wg_ref[d, hh], preferred_element_type=F32)
        tr = jnp.tanh(pre[:, :HALF] + 0.5 * br_ref[d:d + 1, cols])
        ti = jnp.tanh(pre[:, HALF:] + 0.5 * bi_ref[d:d + 1, cols])
        log_a = kk + kk * tr
        a = jnp.exp(log_a)
        t = jnp.tanh(log_a)
        q = t / (t - 1.0)
        gain = q * lax.rsqrt(jnp.maximum(q, TINY))
        bc = gain * ((0.7071067811865476 * xch) * (1.0 + ti))
        a3 = a.reshape(groups, SUBLANES, HALF)
        b3 = bc.reshape(groups, SUBLANES, HALF)
        for s in (1, 2):
            if d == 0:
                sh, m = s, sub >= s
            else:
                sh, m = SUBLANES - s, sub < SUBLANES - s
            am = jnp.where(m, a3, 0.0)
            b3 = b3 + am * pltpu.roll(b3, sh, 1)
            a3 = jnp.where(m, a3 * pltpu.roll(a3, sh, 1), a3)
        c = carry[:, cols]
        for g in order:
            h_near = a3[g] * c + b3[g]
            h_far = a3[g] * pltpu.roll(h_near, SUBLANES // 2, 0) + b3[g]
            h = jnp.where(near, h_near, h_far)
            h_ref[g0 + g, :, cols] = h
            row = h[SUBLANES - 1:SUBLANES, :] if d == 0 else h[0:1, :]
            c = jnp.broadcast_to(row, (SUBLANES, HALF))
        carry_out.append(c)
    return jnp.concatenate(carry_out, axis=1)


def _lru_bwd_kernel(xc_ref, xl_ref, xp_ref, xn_ref, cw_ref, wg_ref, br_ref, bi_ref, lam_ref,
                    h0_ref, hb_ref, xcs_ref, carry_ref, *, ts, nt, tc):
    j = pl.program_id(1)

    @pl.when(j == 0)
    def _():
        carry_ref[...] = jnp.broadcast_to(h0_ref[0], (SUBLANES, LRU_WIDTH))

    _finish_conv(xc_ref, xl_ref, xp_ref, xn_ref, cw_ref, xcs_ref, nt - 1 - j, nt, ts, tc)
    carry = carry_ref[...]
    for r0 in range(ts - tc, -1, -tc):
        carry = _lru_scan(1, xcs_ref[r0:r0 + tc], wg_ref, br_ref, bi_ref, lam_ref, hb_ref,
                          r0 // SUBLANES, carry, tc)
    carry_ref[...] = carry


def _mix_fwd_kernel(xc_ref, xl_ref, xp_ref, xn_ref, gg_ref, u_ref, vn_ref, hb_ref,
                    cw_ref, wg_ref, br_ref, bi_ref, lam_ref, h0_ref, sw_ref, sb_ref,
                    y_ref, hlast_ref, xcs_ref, hf_ref, carry_ref, *, ts, nt, tc):
    j = pl.program_id(1)
    n_chunks = ts // CHUNK

    @pl.when(j == 0)
    def _():
        carry_ref[...] = jnp.broadcast_to(h0_ref[0], (SUBLANES, LRU_WIDTH))

    _finish_conv(xc_ref, xl_ref, xp_ref, xn_ref, cw_ref, xcs_ref, j, nt, ts, tc)
    carry = carry_ref[...]
    for r0 in range(0, ts, tc):
        carry = _lru_scan(0, xcs_ref[r0:r0 + tc], wg_ref, br_ref, bi_ref, lam_ref, hf_ref,
                          r0 // SUBLANES, carry, tc)
    carry_ref[...] = carry
    hlast_ref[0] = carry
    for g in range(MLP_GROUPS):
        cols = slice(g * MLP_GROUP_DIM, (g + 1) * MLP_GROUP_DIM)
        rhs = jnp.concatenate(
            [vn_ref[ch * CHUNK:(ch + 1) * CHUNK, cols] for ch in range(n_chunks)], axis=1)
        z = jnp.dot(sw_ref[g], rhs, preferred_element_type=F32)
        for ch in range(n_chunks):
            rows = slice(ch * CHUNK, (ch + 1) * CHUNK)
            zc = z[:, ch * MLP_GROUP_DIM:(ch + 1) * MLP_GROUP_DIM] + sb_ref[:, cols]
            y_ref[rows, LRU_WIDTH + g * MLP_GROUP_DIM:LRU_WIDTH + (g + 1) * MLP_GROUP_DIM] = (
                u_ref[rows, cols] * zc).astype(BF16)

    h_sum = hf_ref[...].reshape(ts, LRU_WIDTH) + hb_ref[...]
    y_ref[:, 0:LRU_WIDTH] = (h_sum * gg_ref[...]).astype(BF16)


def _window_specs(seq, ts, reverse):
    nt = seq // ts
    rb = ts // SUBLANES
    sb = seq // SUBLANES

    def tidx(j):
        return nt - 1 - j if reverse else j

    return [
        pl.BlockSpec((ts, LRU_WIDTH), lambda b, j: (b * nt + tidx(j), 0)),
        pl.BlockSpec((ts, LRU_WIDTH), lambda b, j: (b * nt + tidx(j), 0)),
        pl.BlockSpec((SUBLANES, LRU_WIDTH),
                     lambda b, j: (b * sb + jnp.maximum(tidx(j) * rb - 1, 0), 0)),
        pl.BlockSpec((SUBLANES, LRU_WIDTH),
                     lambda b, j: (b * sb + jnp.minimum((tidx(j) + 1) * rb, sb - 1), 0)),
    ]


def _mix(xl, xc, gg, u, vn, seq, lru, h0_f, h0_b, sgu_w, sgu_bias, *, ts, tc, ts_bwd=None):
    conv_w, wg, b_r, b_i, lam = lru
    n = xl.shape[0]
    bsz = n // seq
    nt = seq // ts
    groups = ts // SUBLANES
    state_spec = pl.BlockSpec((1, 1, LRU_WIDTH), lambda b, j: (b, 0, 0))
    lru_specs = [
        _resident((4, LRU_WIDTH)),
        _resident((2, N_HALVES, HALF, 2 * HALF)),
        _resident((2, LRU_WIDTH)),
        _resident((2, LRU_WIDTH)),
        _resident((2, LRU_WIDTH)),
    ]
    lru_args = (conv_w, wg, b_r, b_i, lam)
    scan_scratch = [pltpu.VMEM((ts, LRU_WIDTH), F32)]
    carry_scratch = pltpu.VMEM((SUBLANES, LRU_WIDTH), F32)

    def tile_spec(width):
        return pl.BlockSpec((ts, width), lambda b, j: (b * nt + j, 0))

    tsb = ts_bwd or ts
    ntb = seq // tsb
    hb = pl.pallas_call(
        functools.partial(_lru_bwd_kernel, ts=tsb, nt=ntb, tc=tc),
        grid=(bsz, ntb),
        in_specs=_window_specs(seq, tsb, True) + lru_specs + [state_spec],
        out_specs=pl.BlockSpec((tsb // SUBLANES, SUBLANES, LRU_WIDTH),
                               lambda b, j: (b * ntb + ntb - 1 - j, 0, 0)),
        out_shape=jax.ShapeDtypeStruct((n // SUBLANES, SUBLANES, LRU_WIDTH), F32),
        scratch_shapes=[pltpu.VMEM((tsb, LRU_WIDTH), F32), carry_scratch],
        compiler_params=_cparams(2),
        name="lru_bwd",
    )(xc, xl, xl, xl, *lru_args, h0_b)
    hb = hb.reshape(n, LRU_WIDTH)

    y, h_last = pl.pallas_call(
        functools.partial(_mix_fwd_kernel, ts=ts, nt=nt, tc=tc),
        grid=(bsz, nt),
        in_specs=_window_specs(seq, ts, False) + [
            tile_spec(LRU_WIDTH), tile_spec(MLP_WIDTH), tile_spec(MLP_WIDTH), tile_spec(LRU_WIDTH),
        ] + lru_specs + [
            state_spec,
            _resident((MLP_GROUPS, CHUNK, CHUNK)),
            _resident((CHUNK, MLP_WIDTH)),
        ],
        out_specs=[
            tile_spec(D_MODEL),
            pl.BlockSpec((1, SUBLANES, LRU_WIDTH), lambda b, j: (b, 0, 0)),
        ],
        out_shape=[
            jax.ShapeDtypeStruct((n, D_MODEL), BF16),
            jax.ShapeDtypeStruct((bsz, SUBLANES, LRU_WIDTH), F32),
        ],
        scratch_shapes=scan_scratch + [
            pltpu.VMEM((groups, SUBLANES, LRU_WIDTH), F32),
            carry_scratch,
        ],
        compiler_params=_cparams(2),
        name="mix_fwd",
    )(xc, xl, xl, xl, gg, u, vn, hb, *lru_args, h0_f, sgu_w, sgu_bias)
    return y, h_last[:, 0:1, :], hb


def _gate_weights(w_r, w_i):
    hpb = HALF // LRU_HEAD_DIM
    out = []
    for d in range(2):
        halves = []
        for hh in range(N_HALVES):
            blocks_r = [w_r[d, hh * hpb + k] for k in range(hpb)]
            blocks_i = [w_i[d, hh * hpb + k] for k in range(hpb)]
            halves.append(jnp.concatenate(
                [jax.scipy.linalg.block_diag(*blocks_r), jax.scipy.linalg.block_diag(*blocks_i)],
                axis=1))
        out.append(jnp.stack(halves))
    return (0.5 * jnp.stack(out)).astype(BF16)


def kernel(x, c, ctx, c_ctx, w_ada, b_ada, ffn1_norm_g, ffn1_w_in, ffn1_w_out,
           mix_norm_g, w_in_mix, lru_conv_w, lru_conv_b, lru_w_r, lru_b_r, lru_w_i,
           lru_b_i, lru_lambda, sgu_norm_g, sgu_w, sgu_b, w_out_mix,
           ffn2_norm_g, ffn2_w_in, ffn2_w_out, final_norm_g):
    bsz, n_lat, d = x.shape
    n_ctx = ctx.shape[1]
    depth = w_ada.shape[0]
    assert bsz + 1 <= MOD_ROWS and d == D_MODEL
    tm_lat, tm_ctx = 512, n_ctx

    c_rows = jnp.zeros((MOD_ROWS, d), F32).at[:bsz].set(c).at[bsz].set(c_ctx)
    mods = _ada(c_rows, w_ada, b_ada).reshape(depth, MOD_ROWS, N_MOD, d)

    h = x.reshape(bsz * n_lat, d)
    hc = ctx.reshape(bsz * n_ctx, d)
    zeros_state = jnp.zeros((bsz, 1, LRU_WIDTH), F32)
    w1_in, w1_out = _to_bf16(ffn1_w_in), _to_bf16(ffn1_w_out)
    w2_in, w2_out = _to_bf16(ffn2_w_in), _to_bf16(ffn2_w_out)
    w_mix_in, w_mix_out = _to_bf16(w_in_mix), _to_bf16(w_out_mix)

    for l in range(depth):
        last = l == depth - 1
        m = mods[l, :bsz]
        mc = mods[l, bsz:bsz + 1]
        lru = (lru_conv_w[l], _gate_weights(lru_w_r[l], lru_w_i[l]),
               lru_b_r[l], lru_b_i[l], lru_lambda[l])
        s_w = sgu_w[l].astype(BF16)
        s_bias = jnp.repeat(sgu_b[l].T, MLP_GROUP_DIM, axis=1)
        inproj = (mix_norm_g[l], w_mix_in, sgu_norm_g[l], lru_conv_w[l], lru_conv_b[l])

        hc, xl, xc, gg, u, vn = _ffn(hc, mc, 0, ffn1_norm_g[l], w1_in, w1_out, l, tm=tm_ctx,
                                 inproj=inproj)
        yc, hf_last, hb_c = _mix(xl, xc, gg, u, vn, n_ctx, lru, zeros_state, zeros_state,
                                 s_w, s_bias, ts=n_ctx, tc=tm_ctx)
        h0_b = hb_c.reshape(bsz, n_ctx, LRU_WIDTH)[:, 0:1, :]
        if not last:
            hc = _ffn(hc, mc, 6, ffn2_norm_g[l], w2_in, w2_out, l, tm=2 * tm_ctx,
                      outproj=(yc, w_mix_out))

        h, xl, xc, gg, u, vn = _ffn(h, m, 0, ffn1_norm_g[l], w1_in, w1_out, l, tm=tm_lat,
                                inproj=inproj)
        y, _, _ = _mix(xl, xc, gg, u, vn, n_lat, lru, hf_last, h0_b, s_w, s_bias,
                       ts=2 * tm_lat, tc=tm_lat, ts_bwd=4 * tm_lat)
        h = _ffn(h, m, 6, ffn2_norm_g[l], w2_in, w2_out, l, tm=2 * tm_lat,
                 outproj=(y, w_mix_out), final_g=final_norm_g if last else None)

    return h.reshape(bsz, n_lat, d)
```

```python
import functools

import jax
import jax.numpy as jnp
from jax import lax
from jax.experimental import pallas as pl
from jax.experimental.pallas import tpu as pltpu

F32 = jnp.float32
BF16 = jnp.bfloat16

EPS = 1e-6
D_MODEL = 1024
D_FF = 2816
LRU_WIDTH = 512
LRU_HEADS = 8
LRU_HEAD_DIM = 64
RG_C = 8.0
MLP_GROUPS = 4
MLP_GROUP_DIM = 128
MLP_WIDTH = 512
CHUNK = 128
IN_PROJ_WIDTH = 2048
N_MOD = 9
MOD_ROWS = 8
SUBLANES = 8
HALF = 256
N_HALVES = LRU_WIDTH // HALF
TINY = 1e-30
FFN_CHUNK = 256
CAST_BLOCK_ELEMS = 2 * 1024 * 1024

VMEM_LIMIT = 56 * 1024 * 1024


def _cparams(n_axes):
    return pltpu.CompilerParams(
        dimension_semantics=("arbitrary",) * n_axes,
        vmem_limit_bytes=VMEM_LIMIT,
    )


def _resident(shape, layer=None):
    nd = len(shape)
    if layer is None:
        return pl.BlockSpec(shape, lambda *_: (0,) * nd, pipeline_mode=pl.Buffered(1))
    return pl.BlockSpec((None,) + tuple(shape), lambda *_: (layer,) + (0,) * nd,
                        pipeline_mode=pl.Buffered(1))


def _cast_kernel(x_ref, o_ref):
    o_ref[...] = x_ref[...].astype(BF16)


def _to_bf16(w):
    depth, rows, cols = w.shape
    br = max(b for b in range(16, rows + 1, 16)
             if rows % b == 0 and b * cols <= CAST_BLOCK_ELEMS)
    spec = pl.BlockSpec((1, br, cols), lambda l, i: (l, i, 0))
    return pl.pallas_call(
        _cast_kernel,
        grid=(depth, rows // br),
        in_specs=[spec],
        out_specs=spec,
        out_shape=jax.ShapeDtypeStruct(w.shape, BF16),
        compiler_params=_cparams(2),
        name="to_bf16",
    )(w)


def _sigmoid(x):
    return 1.0 / (1.0 + jnp.exp(-x))


def _softplus(x):
    return jnp.maximum(x, 0.0) + jnp.log1p(jnp.exp(-jnp.abs(x)))


def _gelu_tanh(x):
    return 0.5 * x * (1.0 + jnp.tanh(0.7978845608028654 * (x + 0.044715 * (x * x * x))))


def _norm_mod(x, g, shift, scale):
    ms = jnp.mean(x * x, axis=-1, keepdims=True)
    return (x * lax.rsqrt(ms + EPS)) * (g * (1.0 + scale)) + shift


def _ada_kernel(c_ref, w_ref, b_ref, o_ref):
    c = c_ref[...]
    sc = c * _sigmoid(c)
    w = w_ref[0]
    s_hi = sc.astype(BF16)
    s_lo = (sc - s_hi.astype(F32)).astype(BF16)
    w_hi = w.astype(BF16)
    w_lo = (w - w_hi.astype(F32)).astype(BF16)
    p = jnp.dot(jnp.concatenate([s_hi, s_lo], axis=0), w_hi, preferred_element_type=F32)
    p = p[:MOD_ROWS] + p[MOD_ROWS:] + jnp.dot(s_hi, w_lo, preferred_element_type=F32)
    o_ref[0] = p + b_ref[0]


def _ada(c_rows, w_ada, b_ada):
    depth = w_ada.shape[0]
    return pl.pallas_call(
        _ada_kernel,
        grid=(depth, N_MOD),
        in_specs=[
            pl.BlockSpec((MOD_ROWS, D_MODEL), lambda l, j: (0, 0)),
            pl.BlockSpec((1, D_MODEL, D_MODEL), lambda l, j: (l, 0, j)),
            pl.BlockSpec((1, 1, D_MODEL), lambda l, j: (l, 0, j)),
        ],
        out_specs=pl.BlockSpec((1, MOD_ROWS, D_MODEL), lambda l, j: (l, 0, j)),
        out_shape=jax.ShapeDtypeStruct((depth, MOD_ROWS, N_MOD * D_MODEL), F32),
        compiler_params=_cparams(2),
        name="ada_mod",
    )(c_rows, w_ada, b_ada.reshape(depth, 1, N_MOD * D_MODEL))


def _ffn_kernel(*refs, k0, pro, epi):
    refs = list(refs)
    x_ref, mod_ref, g_ref, win_ref, wout_ref = refs[:5]
    pos = 5
    if pro == "outproj":
        y_ref, wo_ref = refs[pos:pos + 2]
        pos += 2
    if epi == "inproj":
        gm_ref, wmix_ref, sg_ref, cw_ref, cb_ref = refs[pos:pos + 5]
        pos += 5
    elif epi == "final":
        fg_ref = refs[pos]
        pos += 1
    o_ref = refs[pos]
    pos += 1
    if epi == "inproj":
        xl_ref, xc_ref, gg_ref, u_ref, vn_ref = refs[pos:pos + 5]
        pos += 5
    act_ref = refs[pos]
    pos += 1
    if epi == "inproj":
        xe_ref = refs[pos]
        tm = x_ref.shape[0]

        @pl.when(pl.program_id(0) == 0)
        def _():
            xe_ref[...] = jnp.zeros_like(xe_ref)

    shift = mod_ref[0, k0:k0 + 1, :]
    scale = mod_ref[0, k0 + 1:k0 + 2, :]
    gate = mod_ref[0, k0 + 2:k0 + 3, :]
    if pro == "outproj":
        hm = x_ref.shape[0] // 2
        hs, zs = [], []
        for r in (slice(0, hm), slice(hm, 2 * hm)):
            hr = x_ref[r, :] + mod_ref[0, 5:6, :] * jnp.dot(y_ref[r, :], wo_ref[...],
                                                            preferred_element_type=F32)
            hs.append(hr)
            zs.append(_norm_mod(hr, g_ref[...], shift, scale).astype(BF16))
        h = jnp.concatenate(hs, axis=0)
        zb = jnp.concatenate(zs, axis=0)
    else:
        h = x_ref[...]
        zb = _norm_mod(h, g_ref[...], shift, scale).astype(BF16)
    for c0 in range(0, D_FF, FFN_CHUNK):
        cw = min(FFN_CHUNK, D_FF - c0)
        g = jnp.dot(zb, win_ref[:, c0:c0 + cw], preferred_element_type=F32)
        u = jnp.dot(zb, win_ref[:, D_FF + c0:D_FF + c0 + cw], preferred_element_type=F32)
        act_ref[:, c0:c0 + cw] = (g * _sigmoid(g) * u).astype(BF16)
    out = h + (0.5 * gate) * jnp.dot(act_ref[...], wout_ref[...], preferred_element_type=F32)
    if epi == "final":
        ms = jnp.mean(out * out, axis=-1, keepdims=True)
        out = out * lax.rsqrt(ms + EPS) * fg_ref[...]
    o_ref[...] = out
    if epi == "inproj":
        zm = _norm_mod(out, gm_ref[...], mod_ref[0, 3:4, :], mod_ref[0, 4:5, :]).astype(BF16)
        v = jnp.dot(zm, wmix_ref[:, 2 * LRU_WIDTH + MLP_WIDTH:], preferred_element_type=F32)
        vc = v - jnp.mean(v, axis=-1, keepdims=True)
        var = jnp.mean(vc * vc, axis=-1, keepdims=True)
        vn_ref[...] = (vc * lax.rsqrt(var + EPS) * sg_ref[...]).astype(BF16)
        xl = jnp.dot(zm, wmix_ref[:, 0:LRU_WIDTH], preferred_element_type=F32)
        xl_ref[...] = xl
        xe_ref[SUBLANES:SUBLANES + tm] = xl
        xc_ref[...] = _short_conv(xe_ref, cw_ref, cb_ref, tm, SUBLANES)
        gl = jnp.dot(zm, wmix_ref[:, LRU_WIDTH:2 * LRU_WIDTH], preferred_element_type=F32)
        gg_ref[...] = _gelu_tanh(gl)
        u_ref[...] = jnp.dot(zm, wmix_ref[:, 2 * LRU_WIDTH:2 * LRU_WIDTH + MLP_WIDTH],
                             preferred_element_type=F32)


def _ffn(x, mod, k0, norm_g, w_in, w_out, layer, *, tm, outproj=None, inproj=None,
         final_g=None):
    n = x.shape[0]
    tiles_per_mod = n // mod.shape[0] // tm
    pro = "outproj" if outproj is not None else None
    epi = "inproj" if inproj is not None else ("final" if final_g is not None else None)

    def row_spec(width):
        return pl.BlockSpec((tm, width), lambda i: (i, 0))

    args = [x, mod, norm_g.reshape(1, D_MODEL), w_in, w_out]
    in_specs = [
        row_spec(D_MODEL),
        pl.BlockSpec((1, N_MOD, D_MODEL), lambda i: (i // tiles_per_mod, 0, 0)),
        _resident((1, D_MODEL)),
        _resident((D_MODEL, 2 * D_FF), layer),
        _resident((D_FF, D_MODEL), layer),
    ]
    out_shape = [jax.ShapeDtypeStruct((n, D_MODEL), F32)]
    out_specs = [row_spec(D_MODEL)]
    if pro == "outproj":
        args += [outproj[0], outproj[1]]
        in_specs += [row_spec(D_MODEL), _resident((D_MODEL, D_MODEL), layer)]
    scratch = [pltpu.VMEM((tm, D_FF), BF16)]
    if epi == "inproj":
        args += [inproj[0].reshape(1, D_MODEL), inproj[1], inproj[2].reshape(1, MLP_WIDTH),
                 inproj[3], inproj[4].reshape(1, LRU_WIDTH)]
        in_specs += [_resident((1, D_MODEL)), _resident((D_MODEL, IN_PROJ_WIDTH), layer),
                     _resident((1, MLP_WIDTH)), _resident((4, LRU_WIDTH)),
                     _resident((1, LRU_WIDTH))]
        out_shape += [jax.ShapeDtypeStruct((n, LRU_WIDTH), F32),
                      jax.ShapeDtypeStruct((n, LRU_WIDTH), F32),
                      jax.ShapeDtypeStruct((n, LRU_WIDTH), F32),
                      jax.ShapeDtypeStruct((n, MLP_WIDTH), F32),
                      jax.ShapeDtypeStruct((n, MLP_WIDTH), BF16)]
        out_specs += [row_spec(LRU_WIDTH), row_spec(LRU_WIDTH), row_spec(LRU_WIDTH),
                      row_spec(MLP_WIDTH), row_spec(MLP_WIDTH)]
        scratch += [pltpu.VMEM((tm + 2 * SUBLANES, LRU_WIDTH), F32)]
    elif epi == "final":
        args += [final_g.reshape(1, D_MODEL)]
        in_specs += [_resident((1, D_MODEL))]
    res = pl.pallas_call(
        functools.partial(_ffn_kernel, k0=k0, pro=pro, epi=epi),
        grid=(n // tm,),
        in_specs=in_specs,
        out_specs=out_specs,
        out_shape=out_shape,
        scratch_shapes=scratch,
        compiler_params=_cparams(1),
        name="ffn_" + (pro or "plain") + "_" + (epi or "plain"),
    )(*args)
    return res if epi == "inproj" else res[0]


def _short_conv(xe_ref, cw_ref, cb_ref, ts, halo):
    xe = xe_ref[...]
    rows = ts + 2 * halo
    xc = cb_ref[...] + xe[halo:halo + ts] * cw_ref[2:3, :]
    for k, sh in ((0, 2), (1, 1), (3, rows - 1)):
        xc = xc + pltpu.roll(xe, sh, 0)[halo:halo + ts] * cw_ref[k:k + 1, :]
    return xc


def _finish_conv(xc_ref, xl_ref, xp_ref, xn_ref, cw_ref, tile, nt, ts, r0, tc):
    zeros = jnp.zeros((SUBLANES, LRU_WIDTH), F32)
    r1 = r0 + tc
    before = jnp.where(tile > 0, xp_ref[...], 0.0) if r0 == 0 else xl_ref[r0 - SUBLANES:r0]
    after = jnp.where(tile < nt - 1, xn_ref[...], 0.0) if r1 == ts else xl_ref[r1:r1 + SUBLANES]
    e = jnp.concatenate([before, zeros], axis=0)
    head = (pltpu.roll(e, 2, 0)[SUBLANES:] * cw_ref[0:1, :]
            + pltpu.roll(e, 1, 0)[SUBLANES:] * cw_ref[1:2, :])
    f = jnp.concatenate([zeros, after], axis=0)
    tail = pltpu.roll(f, 2 * SUBLANES - 1, 0)[0:SUBLANES] * cw_ref[3:4, :]
    return jnp.concatenate([xc_ref[r0:r0 + SUBLANES] + head,
                            xc_ref[r0 + SUBLANES:r1 - SUBLANES],
                            xc_ref[r1 - SUBLANES:r1] + tail], axis=0)


def _lru_scan(d, xc, wg_ref, br_ref, bi_ref, lam_ref, h_ref, g0, carry, ts):
    groups = ts // SUBLANES
    sub = lax.broadcasted_iota(jnp.int32, (groups, SUBLANES, HALF), 1)
    order = range(groups) if d == 0 else range(groups - 1, -1, -1)
    row_id = lax.broadcasted_iota(jnp.int32, (SUBLANES, HALF), 0)
    near = row_id < SUBLANES // 2 if d == 0 else row_id >= SUBLANES // 2
    carry_out = []
    for hh in range(N_HALVES):
        cols = slice(hh * HALF, (hh + 1) * HALF)
        xch = xc[:, cols]
        kk = (-0.5 * RG_C) * _softplus(-lam_ref[d:d + 1, cols])
        pre = jnp.dot(xch.astype(BF16), wg_ref[d, hh], preferred_element_type=F32)
        tr = jnp.tanh(pre[:, :HALF] + 0.5 * br_ref[d:d + 1, cols])
        ti = jnp.tanh(pre[:, HALF:] + 0.5 * bi_ref[d:d + 1, cols])
        log_a = kk + kk * tr
        a = jnp.exp(log_a)
        t = jnp.tanh(log_a)
        q = t / (t - 1.0)
        gain = q * lax.rsqrt(jnp.maximum(q, TINY))
        bc = gain * ((0.7071067811865476 * xch) * (1.0 + ti))
        a3 = a.reshape(groups, SUBLANES, HALF)
        b3 = bc.reshape(groups, SUBLANES, HALF)
        for s in (1, 2):
            if d == 0:
                sh, m = s, sub >= s
            else:
                sh, m = SUBLANES - s, sub < SUBLANES - s
            am = jnp.where(m, a3, 0.0)
            b3 = b3 + am * pltpu.roll(b3, sh, 1)
            a3 = jnp.where(m, a3 * pltpu.roll(a3, sh, 1), a3)
        c = carry[:, cols]
        for g in order:
            h_near = a3[g] * c + b3[g]
            h_far = a3[g] * pltpu.roll(h_near, SUBLANES // 2, 0) + b3[g]
            h = jnp.where(near, h_near, h_far)
            h_ref[g0 + g, :, cols] = h
            row = h[SUBLANES - 1:SUBLANES, :] if d == 0 else h[0:1, :]
            c = jnp.broadcast_to(row, (SUBLANES, HALF))
        carry_out.append(c)
    return jnp.concatenate(carry_out, axis=1)


def _lru_bwd_kernel(xc_ref, xl_ref, xp_ref, xn_ref, cw_ref, wg_ref, br_ref, bi_ref, lam_ref,
                    h0_ref, hb_ref, carry_ref, *, ts, nt, tc):
    j = pl.program_id(1)

    @pl.when(j == 0)
    def _():
        carry_ref[...] = jnp.broadcast_to(h0_ref[0], (SUBLANES, LRU_WIDTH))

    carry = carry_ref[...]
    for r0 in range(ts - tc, -1, -tc):
        xc = _finish_conv(xc_ref, xl_ref, xp_ref, xn_ref, cw_ref, nt - 1 - j, nt, ts, r0, tc)
        carry = _lru_scan(1, xc, wg_ref, br_ref, bi_ref, lam_ref, hb_ref,
                          r0 // SUBLANES, carry, tc)
    carry_ref[...] = carry


def _mix_fwd_kernel(xc_ref, xl_ref, xp_ref, xn_ref, gg_ref, u_ref, vn_ref, hb_ref,
                    cw_ref, wg_ref, br_ref, bi_ref, lam_ref, h0_ref, sw_ref, sb_ref,
                    y_ref, hlast_ref, hf_ref, carry_ref, *, ts, nt, tc):
    j = pl.program_id(1)
    n_chunks = ts // CHUNK

    @pl.when(j == 0)
    def _():
        carry_ref[...] = jnp.broadcast_to(h0_ref[0], (SUBLANES, LRU_WIDTH))

    carry = carry_ref[...]
    for r0 in range(0, ts, tc):
        xc = _finish_conv(xc_ref, xl_ref, xp_ref, xn_ref, cw_ref, j, nt, ts, r0, tc)
        carry = _lru_scan(0, xc, wg_ref, br_ref, bi_ref, lam_ref, hf_ref,
                          r0 // SUBLANES, carry, tc)
    carry_ref[...] = carry
    hlast_ref[0] = carry
    for g in range(MLP_GROUPS):
        cols = slice(g * MLP_GROUP_DIM, (g + 1) * MLP_GROUP_DIM)
        rhs = jnp.concatenate(
            [vn_ref[ch * CHUNK:(ch + 1) * CHUNK, cols] for ch in range(n_chunks)], axis=1)
        z = jnp.dot(sw_ref[g], rhs, preferred_element_type=F32)
        for ch in range(n_chunks):
            rows = slice(ch * CHUNK, (ch + 1) * CHUNK)
            zc = z[:, ch * MLP_GROUP_DIM:(ch + 1) * MLP_GROUP_DIM] + sb_ref[:, cols]
            y_ref[rows, LRU_WIDTH + g * MLP_GROUP_DIM:LRU_WIDTH + (g + 1) * MLP_GROUP_DIM] = (
                u_ref[rows, cols] * zc).astype(BF16)

    h_sum = hf_ref[...].reshape(ts, LRU_WIDTH) + hb_ref[...]
    y_ref[:, 0:LRU_WIDTH] = (h_sum * gg_ref[...]).astype(BF16)


def _window_specs(seq, ts, reverse):
    nt = seq // ts
    rb = ts // SUBLANES
    sb = seq // SUBLANES

    def tidx(j):
        return nt - 1 - j if reverse else j

    return [
        pl.BlockSpec((ts, LRU_WIDTH), lambda b, j: (b * nt + tidx(j), 0)),
        pl.BlockSpec((ts, LRU_WIDTH), lambda b, j: (b * nt + tidx(j), 0)),
        pl.BlockSpec((SUBLANES, LRU_WIDTH),
                     lambda b, j: (b * sb + jnp.maximum(tidx(j) * rb - 1, 0), 0)),
        pl.BlockSpec((SUBLANES, LRU_WIDTH),
                     lambda b, j: (b * sb + jnp.minimum((tidx(j) + 1) * rb, sb - 1), 0)),
    ]


def _mix(xl, xc, gg, u, vn, seq, lru, h0_f, h0_b, sgu_w, sgu_bias, *, ts, tc, ts_bwd=None):
    conv_w, wg, b_r, b_i, lam = lru
    n = xl.shape[0]
    bsz = n // seq
    nt = seq // ts
    groups = ts // SUBLANES
    state_spec = pl.BlockSpec((1, 1, LRU_WIDTH), lambda b, j: (b, 0, 0))
    lru_specs = [
        _resident((4, LRU_WIDTH)),
        _resident((2, N_HALVES, HALF, 2 * HALF)),
        _resident((2, LRU_WIDTH)),
        _resident((2, LRU_WIDTH)),
        _resident((2, LRU_WIDTH)),
    ]
    lru_args = (conv_w, wg, b_r, b_i, lam)
    carry_scratch = pltpu.VMEM((SUBLANES, LRU_WIDTH), F32)

    def tile_spec(width):
        return pl.BlockSpec((ts, width), lambda b, j: (b * nt + j, 0))

    tsb = ts_bwd or ts
    ntb = seq // tsb
    hb = pl.pallas_call(
        functools.partial(_lru_bwd_kernel, ts=tsb, nt=ntb, tc=tc),
        grid=(bsz, ntb),
        in_specs=_window_specs(seq, tsb, True) + lru_specs + [state_spec],
        out_specs=pl.BlockSpec((tsb // SUBLANES, SUBLANES, LRU_WIDTH),
                               lambda b, j: (b * ntb + ntb - 1 - j, 0, 0)),
        out_shape=jax.ShapeDtypeStruct((n // SUBLANES, SUBLANES, LRU_WIDTH), F32),
        scratch_shapes=[carry_scratch],
        compiler_params=_cparams(2),
        name="lru_bwd",
    )(xc, xl, xl, xl, *lru_args, h0_b)
    hb = hb.reshape(n, LRU_WIDTH)

    y, h_last = pl.pallas_call(
        functools.partial(_mix_fwd_kernel, ts=ts, nt=nt, tc=tc),
        grid=(bsz, nt),
        in_specs=_window_specs(seq, ts, False) + [
            tile_spec(LRU_WIDTH), tile_spec(MLP_WIDTH), tile_spec(MLP_WIDTH), tile_spec(LRU_WIDTH),
        ] + lru_specs + [
            state_spec,
            _resident((MLP_GROUPS, CHUNK, CHUNK)),
            _resident((CHUNK, MLP_WIDTH)),
        ],
        out_specs=[
            tile_spec(D_MODEL),
            pl.BlockSpec((1, SUBLANES, LRU_WIDTH), lambda b, j: (b, 0, 0)),
        ],
        out_shape=[
            jax.ShapeDtypeStruct((n, D_MODEL), BF16),
            jax.ShapeDtypeStruct((bsz, SUBLANES, LRU_WIDTH), F32),
        ],
        scratch_shapes=[
            pltpu.VMEM((groups, SUBLANES, LRU_WIDTH), F32),
            carry_scratch,
        ],
        compiler_params=_cparams(2),
        name="mix_fwd",
    )(xc, xl, xl, xl, gg, u, vn, hb, *lru_args, h0_f, sgu_w, sgu_bias)
    return y, h_last[:, 0:1, :], hb


def _gate_weights(w_r, w_i):
    hpb = HALF // LRU_HEAD_DIM
    out = []
    for d in range(2):
        halves = []
        for hh in range(N_HALVES):
            blocks_r = [w_r[d, hh * hpb + k] for k in range(hpb)]
            blocks_i = [w_i[d, hh * hpb + k] for k in range(hpb)]
            halves.append(jnp.concatenate(
                [jax.scipy.linalg.block_diag(*blocks_r), jax.scipy.linalg.block_diag(*blocks_i)],
                axis=1))
        out.append(jnp.stack(halves))
    return (0.5 * jnp.stack(out)).astype(BF16)


def kernel(x, c, ctx, c_ctx, w_ada, b_ada, ffn1_norm_g, ffn1_w_in, ffn1_w_out,
           mix_norm_g, w_in_mix, lru_conv_w, lru_conv_b, lru_w_r, lru_b_r, lru_w_i,
           lru_b_i, lru_lambda, sgu_norm_g, sgu_w, sgu_b, w_out_mix,
           ffn2_norm_g, ffn2_w_in, ffn2_w_out, final_norm_g):
    bsz, n_lat, d = x.shape
    n_ctx = ctx.shape[1]
    depth = w_ada.shape[0]
    assert bsz + 1 <= MOD_ROWS and d == D_MODEL
    tm_lat, tm_ctx = 512, n_ctx

    c_rows = jnp.zeros((MOD_ROWS, d), F32).at[:bsz].set(c).at[bsz].set(c_ctx)
    mods = _ada(c_rows, w_ada, b_ada).reshape(depth, MOD_ROWS, N_MOD, d)

    h = x.reshape(bsz * n_lat, d)
    hc = ctx.reshape(bsz * n_ctx, d)
    zeros_state = jnp.zeros((bsz, 1, LRU_WIDTH), F32)
    w1_in, w1_out = _to_bf16(ffn1_w_in), _to_bf16(ffn1_w_out)
    w2_in, w2_out = _to_bf16(ffn2_w_in), _to_bf16(ffn2_w_out)
    w_mix_in, w_mix_out = _to_bf16(w_in_mix), _to_bf16(w_out_mix)

    for l in range(depth):
        last = l == depth - 1
        m = mods[l, :bsz]
        mc = mods[l, bsz:bsz + 1]
        lru = (lru_conv_w[l], _gate_weights(lru_w_r[l], lru_w_i[l]),
               lru_b_r[l], lru_b_i[l], lru_lambda[l])
        s_w = sgu_w[l].astype(BF16)
        s_bias = jnp.repeat(sgu_b[l].T, MLP_GROUP_DIM, axis=1)
        inproj = (mix_norm_g[l], w_mix_in, sgu_norm_g[l], lru_conv_w[l], lru_conv_b[l])

        hc, xl, xc, gg, u, vn = _ffn(hc, mc, 0, ffn1_norm_g[l], w1_in, w1_out, l, tm=tm_ctx,
                                 inproj=inproj)
        yc, hf_last, hb_c = _mix(xl, xc, gg, u, vn, n_ctx, lru, zeros_state, zeros_state,
                                 s_w, s_bias, ts=n_ctx, tc=tm_ctx)
        h0_b = hb_c.reshape(bsz, n_ctx, LRU_WIDTH)[:, 0:1, :]
        if not last:
            hc = _ffn(hc, mc, 6, ffn2_norm_g[l], w2_in, w2_out, l, tm=2 * tm_ctx,
                      outproj=(yc, w_mix_out))

        h, xl, xc, gg, u, vn = _ffn(h, m, 0, ffn1_norm_g[l], w1_in, w1_out, l, tm=tm_lat,
                                inproj=inproj)
        y, _, _ = _mix(xl, xc, gg, u, vn, n_lat, lru, hf_last, h0_b, s_w, s_bias,
                       ts=2 * tm_lat, tc=tm_lat, ts_bwd=4 * tm_lat)
        h = _ffn(h, m, 6, ffn2_norm_g[l], w2_in, w2_out, l, tm=2 * tm_lat,
                 outproj=(y, w_mix_out), final_g=final_norm_g if last else None)

    return h.reshape(bsz, n_lat, d)
```

```python
import functools

import jax
import jax.numpy as jnp
from jax import lax
from jax.experimental import pallas as pl
from jax.experimental.pallas import tpu as pltpu

F32 = jnp.float32
BF16 = jnp.bfloat16

EPS = 1e-6
D_MODEL = 1024
D_FF = 2816
LRU_WIDTH = 512
LRU_HEADS = 8
LRU_HEAD_DIM = 64
RG_C = 8.0
MLP_GROUPS = 4
MLP_GROUP_DIM = 128
MLP_WIDTH = 512
CHUNK = 128
IN_PROJ_WIDTH = 2048
N_MOD = 9
MOD_ROWS = 8
SUBLANES = 8
HALF = 256
N_HALVES = LRU_WIDTH // HALF
TINY = 1e-30
FFN_CHUNK = 256
CAST_BLOCK_ELEMS = 2 * 1024 * 1024

VMEM_LIMIT = 56 * 1024 * 1024


def _cparams(n_axes):
    return pltpu.CompilerParams(
        dimension_semantics=("arbitrary",) * n_axes,
        vmem_limit_bytes=VMEM_LIMIT,
    )


def _resident(shape, layer=None):
    nd = len(shape)
    if layer is None:
        return pl.BlockSpec(shape, lambda *_: (0,) * nd, pipeline_mode=pl.Buffered(1))
    return pl.BlockSpec((None,) + tuple(shape), lambda *_: (layer,) + (0,) * nd,
                        pipeline_mode=pl.Buffered(1))


def _cast_kernel(x_ref, o_ref):
    o_ref[...] = x_ref[...].astype(BF16)


def _to_bf16(w):
    depth, rows, cols = w.shape
    br = max(b for b in range(16, rows + 1, 16)
             if rows % b == 0 and b * cols <= CAST_BLOCK_ELEMS)
    spec = pl.BlockSpec((1, br, cols), lambda l, i: (l, i, 0))
    return pl.pallas_call(
        _cast_kernel,
        grid=(depth, rows // br),
        in_specs=[spec],
        out_specs=spec,
        out_shape=jax.ShapeDtypeStruct(w.shape, BF16),
        compiler_params=_cparams(2),
        name="to_bf16",
    )(w)


def _sigmoid(x):
    return 1.0 / (1.0 + jnp.exp(-x))


def _softplus(x):
    return jnp.maximum(x, 0.0) + jnp.log1p(jnp.exp(-jnp.abs(x)))


def _gelu_tanh(x):
    return 0.5 * x * (1.0 + jnp.tanh(0.7978845608028654 * (x + 0.044715 * (x * x * x))))


def _norm_mod(x, g, shift, scale):
    ms = jnp.mean(x * x, axis=-1, keepdims=True)
    return (x * lax.rsqrt(ms + EPS)) * (g * (1.0 + scale)) + shift


def _ada_kernel(c_ref, w_ref, b_ref, o_ref):
    c = c_ref[...]
    sc = c * _sigmoid(c)
    w = w_ref[0]
    s_hi = sc.astype(BF16)
    s_lo = (sc - s_hi.astype(F32)).astype(BF16)
    w_hi = w.astype(BF16)
    w_lo = (w - w_hi.astype(F32)).astype(BF16)
    p = jnp.dot(jnp.concatenate([s_hi, s_lo], axis=0), w_hi, preferred_element_type=F32)
    p = p[:MOD_ROWS] + p[MOD_ROWS:] + jnp.dot(s_hi, w_lo, preferred_element_type=F32)
    o_ref[0] = p + b_ref[0]


def _ada(c_rows, w_ada, b_ada):
    depth = w_ada.shape[0]
    return pl.pallas_call(
        _ada_kernel,
        grid=(depth, N_MOD),
        in_specs=[
            pl.BlockSpec((MOD_ROWS, D_MODEL), lambda l, j: (0, 0)),
            pl.BlockSpec((1, D_MODEL, D_MODEL), lambda l, j: (l, 0, j)),
            pl.BlockSpec((1, 1, D_MODEL), lambda l, j: (l, 0, j)),
        ],
        out_specs=pl.BlockSpec((1, MOD_ROWS, D_MODEL), lambda l, j: (l, 0, j)),
        out_shape=jax.ShapeDtypeStruct((depth, MOD_ROWS, N_MOD * D_MODEL), F32),
        compiler_params=_cparams(2),
        name="ada_mod",
    )(c_rows, w_ada, b_ada.reshape(depth, 1, N_MOD * D_MODEL))


def _ffn_kernel(*refs, k0, pro, epi):
    refs = list(refs)
    x_ref, mod_ref, g_ref, win_ref, wout_ref = refs[:5]
    pos = 5
    if pro == "outproj":
        y_ref, wo_ref = refs[pos:pos + 2]
        pos += 2
    if epi == "inproj":
        gm_ref, wmix_ref, sg_ref, cw_ref, cb_ref = refs[pos:pos + 5]
        pos += 5
    elif epi == "final":
        fg_ref = refs[pos]
        pos += 1
    o_ref = refs[pos]
    pos += 1
    if epi == "inproj":
        xl_ref, xc_ref, gg_ref, u_ref, vn_ref = refs[pos:pos + 5]
        pos += 5
    act_ref = refs[pos]
    pos += 1
    if epi == "inproj":
        xe_ref = refs[pos]
        tm = x_ref.shape[0]

        @pl.when(pl.program_id(0) == 0)
        def _():
            xe_ref[...] = jnp.zeros_like(xe_ref)

    shift = mod_ref[0, k0:k0 + 1, :]
    scale = mod_ref[0, k0 + 1:k0 + 2, :]
    gate = mod_ref[0, k0 + 2:k0 + 3, :]
    if pro == "outproj":
        hm = x_ref.shape[0] // 2
        hs, zs = [], []
        for r in (slice(0, hm), slice(hm, 2 * hm)):
            hr = x_ref[r, :] + mod_ref[0, 5:6, :] * jnp.dot(y_ref[r, :], wo_ref[...],
                                                            preferred_element_type=F32)
            hs.append(hr)
            zs.append(_norm_mod(hr, g_ref[...], shift, scale).astype(BF16))
        h = jnp.concatenate(hs, axis=0)
        zb = jnp.concatenate(zs, axis=0)
    else:
        h = x_ref[...]
        zb = _norm_mod(h, g_ref[...], shift, scale).astype(BF16)
    for c0 in range(0, D_FF, FFN_CHUNK):
        cw = min(FFN_CHUNK, D_FF - c0)
        g = jnp.dot(zb, win_ref[:, c0:c0 + cw], preferred_element_type=F32)
        u = jnp.dot(zb, win_ref[:, D_FF + c0:D_FF + c0 + cw], preferred_element_type=F32)
        act_ref[:, c0:c0 + cw] = (g * _sigmoid(g) * u).astype(BF16)
    out = h + (0.5 * gate) * jnp.dot(act_ref[...], wout_ref[...], preferred_element_type=F32)
    if epi == "final":
        ms = jnp.mean(out * out, axis=-1, keepdims=True)
        out = out * lax.rsqrt(ms + EPS) * fg_ref[...]
    o_ref[...] = out
    if epi == "inproj":
        zm = _norm_mod(out, gm_ref[...], mod_ref[0, 3:4, :], mod_ref[0, 4:5, :]).astype(BF16)
        v = jnp.dot(zm, wmix_ref[:, 2 * LRU_WIDTH + MLP_WIDTH:], preferred_element_type=F32)
        vc = v - jnp.mean(v, axis=-1, keepdims=True)
        var = jnp.mean(vc * vc, axis=-1, keepdims=True)
        vn_ref[...] = (vc * lax.rsqrt(var + EPS) * sg_ref[...]).astype(BF16)
        xl = jnp.dot(zm, wmix_ref[:, 0:LRU_WIDTH], preferred_element_type=F32)
        xl_ref[...] = xl
        xe_ref[SUBLANES:SUBLANES + tm] = xl
        xc_ref[...] = _short_conv(xe_ref, cw_ref, cb_ref, tm, SUBLANES)
        gl = jnp.dot(zm, wmix_ref[:, LRU_WIDTH:2 * LRU_WIDTH], preferred_element_type=F32)
        gg_ref[...] = _gelu_tanh(gl)
        u_ref[...] = jnp.dot(zm, wmix_ref[:, 2 * LRU_WIDTH:2 * LRU_WIDTH + MLP_WIDTH],
                             preferred_element_type=F32)


def _ffn(x, mod, k0, norm_g, w_in, w_out, layer, *, tm, outproj=None, inproj=None,
         final_g=None):
    n = x.shape[0]
    tiles_per_mod = n // mod.shape[0] // tm
    pro = "outproj" if outproj is not None else None
    epi = "inproj" if inproj is not None else ("final" if final_g is not None else None)

    def row_spec(width):
        return pl.BlockSpec((tm, width), lambda i: (i, 0))

    args = [x, mod, norm_g.reshape(1, D_MODEL), w_in, w_out]
    in_specs = [
        row_spec(D_MODEL),
        pl.BlockSpec((1, N_MOD, D_MODEL), lambda i: (i // tiles_per_mod, 0, 0)),
        _resident((1, D_MODEL)),
        _resident((D_MODEL, 2 * D_FF), layer),
        _resident((D_FF, D_MODEL), layer),
    ]
    out_shape = [jax.ShapeDtypeStruct((n, D_MODEL), F32)]
    out_specs = [row_spec(D_MODEL)]
    if pro == "outproj":
        args += [outproj[0], outproj[1]]
        in_specs += [row_spec(D_MODEL), _resident((D_MODEL, D_MODEL), layer)]
    scratch = [pltpu.VMEM((tm, D_FF), BF16)]
    if epi == "inproj":
        args += [inproj[0].reshape(1, D_MODEL), inproj[1], inproj[2].reshape(1, MLP_WIDTH),
                 inproj[3], inproj[4].reshape(1, LRU_WIDTH)]
        in_specs += [_resident((1, D_MODEL)), _resident((D_MODEL, IN_PROJ_WIDTH), layer),
                     _resident((1, MLP_WIDTH)), _resident((4, LRU_WIDTH)),
                     _resident((1, LRU_WIDTH))]
        out_shape += [jax.ShapeDtypeStruct((n, LRU_WIDTH), F32),
                      jax.ShapeDtypeStruct((n, LRU_WIDTH), F32),
                      jax.ShapeDtypeStruct((n, LRU_WIDTH), F32),
                      jax.ShapeDtypeStruct((n, MLP_WIDTH), F32),
                      jax.ShapeDtypeStruct((n, MLP_WIDTH), BF16)]
        out_specs += [row_spec(LRU_WIDTH), row_spec(LRU_WIDTH), row_spec(LRU_WIDTH),
                      row_spec(MLP_WIDTH), row_spec(MLP_WIDTH)]
        scratch += [pltpu.VMEM((tm + 2 * SUBLANES, LRU_WIDTH), F32)]
    elif epi == "final":
        args += [final_g.reshape(1, D_MODEL)]
        in_specs += [_resident((1, D_MODEL))]
    res = pl.pallas_call(
        functools.partial(_ffn_kernel, k0=k0, pro=pro, epi=epi),
        grid=(n // tm,),
        in_specs=in_specs,
        out_specs=out_specs,
        out_shape=out_shape,
        scratch_shapes=scratch,
        compiler_params=_cparams(1),
        name="ffn_" + (pro or "plain") + "_" + (epi or "plain"),
    )(*args)
    return res if epi == "inproj" else res[0]


def _short_conv(xe_ref, cw_ref, cb_ref, ts, halo):
    xe = xe_ref[...]
    rows = ts + 2 * halo
    xc = cb_ref[...] + xe[halo:halo + ts] * cw_ref[2:3, :]
    for k, sh in ((0, 2), (1, 1), (3, rows - 1)):
        xc = xc + pltpu.roll(xe, sh, 0)[halo:halo + ts] * cw_ref[k:k + 1, :]
    return xc


def _finish_conv(xc_ref, xi_refs, xp_ref, xn_ref, cw_ref, tile, nt, ts, r0, tc):
    zeros = jnp.zeros((SUBLANES, LRU_WIDTH), F32)
    r1 = r0 + tc
    before = (jnp.where(tile > 0, xp_ref[...], 0.0) if r0 == 0
              else xi_refs[2 * (r0 // tc - 1)][...])
    after = (jnp.where(tile < nt - 1, xn_ref[...], 0.0) if r1 == ts
             else xi_refs[2 * (r1 // tc - 1) + 1][...])
    e = jnp.concatenate([before, zeros], axis=0)
    head = (pltpu.roll(e, 2, 0)[SUBLANES:] * cw_ref[0:1, :]
            + pltpu.roll(e, 1, 0)[SUBLANES:] * cw_ref[1:2, :])
    f = jnp.concatenate([zeros, after], axis=0)
    tail = pltpu.roll(f, 2 * SUBLANES - 1, 0)[0:SUBLANES] * cw_ref[3:4, :]
    return jnp.concatenate([xc_ref[r0:r0 + SUBLANES] + head,
                            xc_ref[r0 + SUBLANES:r1 - SUBLANES],
                            xc_ref[r1 - SUBLANES:r1] + tail], axis=0)


def _lru_scan(d, xc, wg_ref, br_ref, bi_ref, lam_ref, h_ref, g0, carry, ts):
    groups = ts // SUBLANES
    sub = lax.broadcasted_iota(jnp.int32, (groups, SUBLANES, HALF), 1)
    order = range(groups) if d == 0 else range(groups - 1, -1, -1)
    row_id = lax.broadcasted_iota(jnp.int32, (SUBLANES, HALF), 0)
    near = row_id < SUBLANES // 2 if d == 0 else row_id >= SUBLANES // 2
    carry_out = []
    for hh in range(N_HALVES):
        cols = slice(hh * HALF, (hh + 1) * HALF)
        xch = xc[:, cols]
        kk = (-0.5 * RG_C) * _softplus(-lam_ref[d:d + 1, cols])
        pre = jnp.dot(xch.astype(BF16), wg_ref[d, hh], preferred_element_type=F32)
        tr = jnp.tanh(pre[:, :HALF] + 0.5 * br_ref[d:d + 1, cols])
        ti = jnp.tanh(pre[:, HALF:] + 0.5 * bi_ref[d:d + 1, cols])
        log_a = kk + kk * tr
        a = jnp.exp(log_a)
        t = jnp.tanh(log_a)
        q = t / (t - 1.0)
        gain = q * lax.rsqrt(jnp.maximum(q, TINY))
        bc = gain * ((0.7071067811865476 * xch) * (1.0 + ti))
        a3 = a.reshape(groups, SUBLANES, HALF)
        b3 = bc.reshape(groups, SUBLANES, HALF)
        for s in (1, 2):
            if d == 0:
                sh, m = s, sub >= s
            else:
                sh, m = SUBLANES - s, sub < SUBLANES - s
            am = jnp.where(m, a3, 0.0)
            b3 = b3 + am * pltpu.roll(b3, sh, 1)
            a3 = jnp.where(m, a3 * pltpu.roll(a3, sh, 1), a3)
        c = carry[:, cols]
        for g in order:
            h_near = a3[g] * c + b3[g]
            h_far = a3[g] * pltpu.roll(h_near, SUBLANES // 2, 0) + b3[g]
            h = jnp.where(near, h_near, h_far)
            h_ref[g0 + g, :, cols] = h
            row = h[SUBLANES - 1:SUBLANES, :] if d == 0 else h[0:1, :]
            c = jnp.broadcast_to(row, (SUBLANES, HALF))
        carry_out.append(c)
    return jnp.concatenate(carry_out, axis=1)


def _lru_bwd_kernel(*refs, ts, nt, tc):
    n_int = 2 * (ts // tc - 1)
    xc_ref, xp_ref, xn_ref = refs[:3]
    xi_refs = refs[3:3 + n_int]
    cw_ref, wg_ref, br_ref, bi_ref, lam_ref, h0_ref, hb_ref, carry_ref = refs[3 + n_int:]
    j = pl.program_id(1)

    @pl.when(j == 0)
    def _():
        carry_ref[...] = jnp.broadcast_to(h0_ref[0], (SUBLANES, LRU_WIDTH))

    carry = carry_ref[...]
    for r0 in range(ts - tc, -1, -tc):
        xc = _finish_conv(xc_ref, xi_refs, xp_ref, xn_ref, cw_ref, nt - 1 - j, nt, ts, r0, tc)
        carry = _lru_scan(1, xc, wg_ref, br_ref, bi_ref, lam_ref, hb_ref,
                          r0 // SUBLANES, carry, tc)
    carry_ref[...] = carry


def _mix_fwd_kernel(*refs, ts, nt, tc):
    n_int = 2 * (ts // tc - 1)
    xc_ref, xp_ref, xn_ref = refs[:3]
    xi_refs = refs[3:3 + n_int]
    (gg_ref, u_ref, vn_ref, hb_ref, cw_ref, wg_ref, br_ref, bi_ref, lam_ref, h0_ref, sw_ref,
     sb_ref, y_ref, hlast_ref, hf_ref, carry_ref) = refs[3 + n_int:]
    j = pl.program_id(1)
    n_chunks = ts // CHUNK

    @pl.when(j == 0)
    def _():
        carry_ref[...] = jnp.broadcast_to(h0_ref[0], (SUBLANES, LRU_WIDTH))

    carry = carry_ref[...]
    for r0 in range(0, ts, tc):
        xc = _finish_conv(xc_ref, xi_refs, xp_ref, xn_ref, cw_ref, j, nt, ts, r0, tc)
        carry = _lru_scan(0, xc, wg_ref, br_ref, bi_ref, lam_ref, hf_ref,
                          r0 // SUBLANES, carry, tc)
    carry_ref[...] = carry
    hlast_ref[0] = carry
    for g in range(MLP_GROUPS):
        cols = slice(g * MLP_GROUP_DIM, (g + 1) * MLP_GROUP_DIM)
        rhs = jnp.concatenate(
            [vn_ref[ch * CHUNK:(ch + 1) * CHUNK, cols] for ch in range(n_chunks)], axis=1)
        z = jnp.dot(sw_ref[g], rhs, preferred_element_type=F32)
        for ch in range(n_chunks):
            rows = slice(ch * CHUNK, (ch + 1) * CHUNK)
            zc = z[:, ch * MLP_GROUP_DIM:(ch + 1) * MLP_GROUP_DIM] + sb_ref[:, cols]
            y_ref[rows, LRU_WIDTH + g * MLP_GROUP_DIM:LRU_WIDTH + (g + 1) * MLP_GROUP_DIM] = (
                u_ref[rows, cols] * zc).astype(BF16)

    h_sum = hf_ref[...].reshape(ts, LRU_WIDTH) + hb_ref[...]
    y_ref[:, 0:LRU_WIDTH] = (h_sum * gg_ref[...]).astype(BF16)


def _window_specs(seq, ts, tc, reverse):
    nt = seq // ts
    rb = ts // SUBLANES
    sb = seq // SUBLANES

    def tidx(j):
        return nt - 1 - j if reverse else j

    def rows8(offset):
        return pl.BlockSpec((SUBLANES, LRU_WIDTH),
                            lambda b, j: (b * sb + tidx(j) * rb + offset, 0))

    specs = [
        pl.BlockSpec((ts, LRU_WIDTH), lambda b, j: (b * nt + tidx(j), 0)),
        pl.BlockSpec((SUBLANES, LRU_WIDTH),
                     lambda b, j: (b * sb + jnp.maximum(tidx(j) * rb - 1, 0), 0)),
        pl.BlockSpec((SUBLANES, LRU_WIDTH),
                     lambda b, j: (b * sb + jnp.minimum((tidx(j) + 1) * rb, sb - 1), 0)),
    ]
    for r in range(tc, ts, tc):
        specs += [rows8(r // SUBLANES - 1), rows8(r // SUBLANES)]
    return specs


def _mix(xl, xc, gg, u, vn, seq, lru, h0_f, h0_b, sgu_w, sgu_bias, *, ts, tc, ts_bwd=None):
    conv_w, wg, b_r, b_i, lam = lru
    n = xl.shape[0]
    bsz = n // seq
    nt = seq // ts
    groups = ts // SUBLANES
    state_spec = pl.BlockSpec((1, 1, LRU_WIDTH), lambda b, j: (b, 0, 0))
    lru_specs = [
        _resident((4, LRU_WIDTH)),
        _resident((2, N_HALVES, HALF, 2 * HALF)),
        _resident((2, LRU_WIDTH)),
        _resident((2, LRU_WIDTH)),
        _resident((2, LRU_WIDTH)),
    ]
    lru_args = (conv_w, wg, b_r, b_i, lam)
    carry_scratch = pltpu.VMEM((SUBLANES, LRU_WIDTH), F32)

    def tile_spec(width):
        return pl.BlockSpec((ts, width), lambda b, j: (b * nt + j, 0))

    tsb = ts_bwd or ts
    ntb = seq // tsb
    hb = pl.pallas_call(
        functools.partial(_lru_bwd_kernel, ts=tsb, nt=ntb, tc=tc),
        grid=(bsz, ntb),
        in_specs=_window_specs(seq, tsb, tc, True) + lru_specs + [state_spec],
        out_specs=pl.BlockSpec((tsb // SUBLANES, SUBLANES, LRU_WIDTH),
                               lambda b, j: (b * ntb + ntb - 1 - j, 0, 0)),
        out_shape=jax.ShapeDtypeStruct((n // SUBLANES, SUBLANES, LRU_WIDTH), F32),
        scratch_shapes=[carry_scratch],
        compiler_params=_cparams(2),
        name="lru_bwd",
    )(xc, *[xl] * (2 * (tsb // tc)), *lru_args, h0_b)
    hb = hb.reshape(n, LRU_WIDTH)

    y, h_last = pl.pallas_call(
        functools.partial(_mix_fwd_kernel, ts=ts, nt=nt, tc=tc),
        grid=(bsz, nt),
        in_specs=_window_specs(seq, ts, tc, False) + [
            tile_spec(LRU_WIDTH), tile_spec(MLP_WIDTH), tile_spec(MLP_WIDTH), tile_spec(LRU_WIDTH),
        ] + lru_specs + [
            state_spec,
            _resident((MLP_GROUPS, CHUNK, CHUNK)),
            _resident((CHUNK, MLP_WIDTH)),
        ],
        out_specs=[
            tile_spec(D_MODEL),
            pl.BlockSpec((1, SUBLANES, LRU_WIDTH), lambda b, j: (b, 0, 0)),
        ],
        out_shape=[
            jax.ShapeDtypeStruct((n, D_MODEL), BF16),
            jax.ShapeDtypeStruct((bsz, SUBLANES, LRU_WIDTH), F32),
        ],
        scratch_shapes=[
            pltpu.VMEM((groups, SUBLANES, LRU_WIDTH), F32),
            carry_scratch,
        ],
        compiler_params=_cparams(2),
        name="mix_fwd",
    )(xc, *[xl] * (2 * (ts // tc)), gg, u, vn, hb, *lru_args, h0_f, sgu_w, sgu_bias)
    return y, h_last[:, 0:1, :], hb


def _gate_weights(w_r, w_i):
    hpb = HALF // LRU_HEAD_DIM
    out = []
    for d in range(2):
        halves = []
        for hh in range(N_HALVES):
            blocks_r = [w_r[d, hh * hpb + k] for k in range(hpb)]
            blocks_i = [w_i[d, hh * hpb + k] for k in range(hpb)]
            halves.append(jnp.concatenate(
                [jax.scipy.linalg.block_diag(*blocks_r), jax.scipy.linalg.block_diag(*blocks_i)],
                axis=1))
        out.append(jnp.stack(halves))
    return (0.5 * jnp.stack(out)).astype(BF16)


def kernel(x, c, ctx, c_ctx, w_ada, b_ada, ffn1_norm_g, ffn1_w_in, ffn1_w_out,
           mix_norm_g, w_in_mix, lru_conv_w, lru_conv_b, lru_w_r, lru_b_r, lru_w_i,
           lru_b_i, lru_lambda, sgu_norm_g, sgu_w, sgu_b, w_out_mix,
           ffn2_norm_g, ffn2_w_in, ffn2_w_out, final_norm_g):
    bsz, n_lat, d = x.shape
    n_ctx = ctx.shape[1]
    depth = w_ada.shape[0]
    assert bsz + 1 <= MOD_ROWS and d == D_MODEL
    tm_lat, tm_ctx = 512, n_ctx

    c_rows = jnp.zeros((MOD_ROWS, d), F32).at[:bsz].set(c).at[bsz].set(c_ctx)
    mods = _ada(c_rows, w_ada, b_ada).reshape(depth, MOD_ROWS, N_MOD, d)

    h = x.reshape(bsz * n_lat, d)
    hc = ctx.reshape(bsz * n_ctx, d)
    zeros_state = jnp.zeros((bsz, 1, LRU_WIDTH), F32)
    w1_in, w1_out = _to_bf16(ffn1_w_in), _to_bf16(ffn1_w_out)
    w2_in, w2_out = _to_bf16(ffn2_w_in), _to_bf16(ffn2_w_out)
    w_mix_in, w_mix_out = _to_bf16(w_in_mix), _to_bf16(w_out_mix)

    for l in range(depth):
        last = l == depth - 1
        m = mods[l, :bsz]
        mc = mods[l, bsz:bsz + 1]
        lru = (lru_conv_w[l], _gate_weights(lru_w_r[l], lru_w_i[l]),
               lru_b_r[l], lru_b_i[l], lru_lambda[l])
        s_w = sgu_w[l].astype(BF16)
        s_bias = jnp.repeat(sgu_b[l].T, MLP_GROUP_DIM, axis=1)
        inproj = (mix_norm_g[l], w_mix_in, sgu_norm_g[l], lru_conv_w[l], lru_conv_b[l])

        hc, xl, xc, gg, u, vn = _ffn(hc, mc, 0, ffn1_norm_g[l], w1_in, w1_out, l, tm=tm_ctx,
                                 inproj=inproj)
        yc, hf_last, hb_c = _mix(xl, xc, gg, u, vn, n_ctx, lru, zeros_state, zeros_state,
                                 s_w, s_bias, ts=n_ctx, tc=tm_ctx)
        h0_b = hb_c.reshape(bsz, n_ctx, LRU_WIDTH)[:, 0:1, :]
        if not last:
            hc = _ffn(hc, mc, 6, ffn2_norm_g[l], w2_in, w2_out, l, tm=2 * tm_ctx,
                      outproj=(yc, w_mix_out))

        h, xl, xc, gg, u, vn = _ffn(h, m, 0, ffn1_norm_g[l], w1_in, w1_out, l, tm=tm_lat,
                                inproj=inproj)
        y, _, _ = _mix(xl, xc, gg, u, vn, n_lat, lru, hf_last, h0_b, s_w, s_bias,
                       ts=2 * tm_lat, tc=tm_lat, ts_bwd=4 * tm_lat)
        h = _ffn(h, m, 6, ffn2_norm_g[l], w2_in, w2_out, l, tm=2 * tm_lat,
                 outproj=(y, w_mix_out), final_g=final_norm_g if last else None)

    return h.reshape(bsz, n_lat, d)
```

```python
import functools

import jax
import jax.numpy as jnp
from jax import lax
from jax.experimental import pallas as pl
from jax.experimental.pallas import tpu as pltpu

F32 = jnp.float32
BF16 = jnp.bfloat16

EPS = 1e-6
D_MODEL = 1024
D_FF = 2816
LRU_WIDTH = 512
LRU_HEADS = 8
LRU_HEAD_DIM = 64
RG_C = 8.0
MLP_GROUPS = 4
MLP_GROUP_DIM = 128
MLP_WIDTH = 512
CHUNK = 128
IN_PROJ_WIDTH = 2048
N_MOD = 9
MOD_ROWS = 8
SUBLANES = 8
HALF = 256
N_HALVES = LRU_WIDTH // HALF
TINY = 1e-30
FFN_CHUNK = 256
CAST_BLOCK_ELEMS = 2 * 1024 * 1024

VMEM_LIMIT = 56 * 1024 * 1024


def _cparams(n_axes):
    return pltpu.CompilerParams(
        dimension_semantics=("arbitrary",) * n_axes,
        vmem_limit_bytes=VMEM_LIMIT,
    )


def _resident(shape, layer=None):
    nd = len(shape)
    if layer is None:
        return pl.BlockSpec(shape, lambda *_: (0,) * nd, pipeline_mode=pl.Buffered(1))
    return pl.BlockSpec((None,) + tuple(shape), lambda *_: (layer,) + (0,) * nd,
                        pipeline_mode=pl.Buffered(1))


def _cast_kernel(x_ref, o_ref):
    o_ref[...] = x_ref[...].astype(BF16)


def _to_bf16(w):
    depth, rows, cols = w.shape
    br = max(b for b in range(16, rows + 1, 16)
             if rows % b == 0 and b * cols <= CAST_BLOCK_ELEMS)
    spec = pl.BlockSpec((1, br, cols), lambda l, i: (l, i, 0))
    return pl.pallas_call(
        _cast_kernel,
        grid=(depth, rows // br),
        in_specs=[spec],
        out_specs=spec,
        out_shape=jax.ShapeDtypeStruct(w.shape, BF16),
        compiler_params=_cparams(2),
        name="to_bf16",
    )(w)


def _sigmoid(x):
    return 1.0 / (1.0 + jnp.exp(-x))


def _softplus(x):
    return jnp.maximum(x, 0.0) + jnp.log1p(jnp.exp(-jnp.abs(x)))


def _gelu_tanh(x):
    return 0.5 * x * (1.0 + jnp.tanh(0.7978845608028654 * (x + 0.044715 * (x * x * x))))


def _norm_mod(x, g, shift, scale):
    ms = jnp.mean(x * x, axis=-1, keepdims=True)
    return (x * lax.rsqrt(ms + EPS)) * (g * (1.0 + scale)) + shift


def _ada_kernel(c_ref, w_ref, b_ref, o_ref):
    c = c_ref[...]
    sc = c * _sigmoid(c)
    w = w_ref[0]
    s_hi = sc.astype(BF16)
    s_lo = (sc - s_hi.astype(F32)).astype(BF16)
    w_hi = w.astype(BF16)
    w_lo = (w - w_hi.astype(F32)).astype(BF16)
    p = jnp.dot(jnp.concatenate([s_hi, s_lo], axis=0), w_hi, preferred_element_type=F32)
    p = p[:MOD_ROWS] + p[MOD_ROWS:] + jnp.dot(s_hi, w_lo, preferred_element_type=F32)
    o_ref[0] = p + b_ref[0]


def _ada(c_rows, w_ada, b_ada):
    depth = w_ada.shape[0]
    return pl.pallas_call(
        _ada_kernel,
        grid=(depth, N_MOD),
        in_specs=[
            pl.BlockSpec((MOD_ROWS, D_MODEL), lambda l, j: (0, 0)),
            pl.BlockSpec((1, D_MODEL, D_MODEL), lambda l, j: (l, 0, j)),
            pl.BlockSpec((1, 1, D_MODEL), lambda l, j: (l, 0, j)),
        ],
        out_specs=pl.BlockSpec((1, MOD_ROWS, D_MODEL), lambda l, j: (l, 0, j)),
        out_shape=jax.ShapeDtypeStruct((depth, MOD_ROWS, N_MOD * D_MODEL), F32),
        compiler_params=_cparams(2),
        name="ada_mod",
    )(c_rows, w_ada, b_ada.reshape(depth, 1, N_MOD * D_MODEL))


def _ffn_kernel(*refs, k0, pro, epi):
    refs = list(refs)
    x_ref, mod_ref, g_ref, win_ref, wout_ref = refs[:5]
    pos = 5
    if pro == "outproj":
        y_ref, wo_ref = refs[pos:pos + 2]
        pos += 2
    if epi == "inproj":
        gm_ref, wmix_ref, sg_ref, cw_ref, cb_ref = refs[pos:pos + 5]
        pos += 5
    elif epi == "final":
        fg_ref = refs[pos]
        pos += 1
    o_ref = refs[pos]
    pos += 1
    if epi == "inproj":
        p_ref, vn_ref = refs[pos:pos + 2]
        pos += 2
    act_ref = refs[pos]
    pos += 1
    if epi == "inproj":
        xe_ref = refs[pos]
        tm = x_ref.shape[0]

        @pl.when(pl.program_id(0) == 0)
        def _():
            xe_ref[...] = jnp.zeros_like(xe_ref)

    shift = mod_ref[0, k0:k0 + 1, :]
    scale = mod_ref[0, k0 + 1:k0 + 2, :]
    gate = mod_ref[0, k0 + 2:k0 + 3, :]
    if pro == "outproj":
        hm = x_ref.shape[0] // 2
        hs, zs = [], []
        for r in (slice(0, hm), slice(hm, 2 * hm)):
            hr = x_ref[r, :] + mod_ref[0, 5:6, :] * jnp.dot(y_ref[r, :], wo_ref[...],
                                                            preferred_element_type=F32)
            hs.append(hr)
            zs.append(_norm_mod(hr, g_ref[...], shift, scale).astype(BF16))
        h = jnp.concatenate(hs, axis=0)
        zb = jnp.concatenate(zs, axis=0)
    else:
        h = x_ref[...]
        zb = _norm_mod(h, g_ref[...], shift, scale).astype(BF16)
    for c0 in range(0, D_FF, FFN_CHUNK):
        cw = min(FFN_CHUNK, D_FF - c0)
        g = jnp.dot(zb, win_ref[:, c0:c0 + cw], preferred_element_type=F32)
        u = jnp.dot(zb, win_ref[:, D_FF + c0:D_FF + c0 + cw], preferred_element_type=F32)
        act_ref[:, c0:c0 + cw] = (g * _sigmoid(g) * u).astype(BF16)
    out = h + (0.5 * gate) * jnp.dot(act_ref[...], wout_ref[...], preferred_element_type=F32)
    if epi == "final":
        ms = jnp.mean(out * out, axis=-1, keepdims=True)
        out = out * lax.rsqrt(ms + EPS) * fg_ref[...]
    o_ref[...] = out
    if epi == "inproj":
        zm = _norm_mod(out, gm_ref[...], mod_ref[0, 3:4, :], mod_ref[0, 4:5, :]).astype(BF16)
        v = jnp.dot(zm, wmix_ref[:, 2 * LRU_WIDTH + MLP_WIDTH:], preferred_element_type=F32)
        vc = v - jnp.mean(v, axis=-1, keepdims=True)
        var = jnp.mean(vc * vc, axis=-1, keepdims=True)
        vn_ref[...] = (vc * lax.rsqrt(var + EPS) * sg_ref[...]).astype(BF16)
        xl = jnp.dot(zm, wmix_ref[:, 0:LRU_WIDTH], preferred_element_type=F32)
        p_ref[:, 0:LRU_WIDTH] = xl
        xe_ref[SUBLANES:SUBLANES + tm] = xl
        p_ref[:, LRU_WIDTH:2 * LRU_WIDTH] = _short_conv(xe_ref, cw_ref, cb_ref, tm, SUBLANES)
        gl = jnp.dot(zm, wmix_ref[:, LRU_WIDTH:2 * LRU_WIDTH], preferred_element_type=F32)
        p_ref[:, 2 * LRU_WIDTH:3 * LRU_WIDTH] = _gelu_tanh(gl)
        p_ref[:, 3 * LRU_WIDTH:] = jnp.dot(
            zm, wmix_ref[:, 2 * LRU_WIDTH:2 * LRU_WIDTH + MLP_WIDTH], preferred_element_type=F32)


def _ffn(x, mod, k0, norm_g, w_in, w_out, layer, *, tm, outproj=None, inproj=None,
         final_g=None):
    n = x.shape[0]
    tiles_per_mod = n // mod.shape[0] // tm
    pro = "outproj" if outproj is not None else None
    epi = "inproj" if inproj is not None else ("final" if final_g is not None else None)

    def row_spec(width):
        return pl.BlockSpec((tm, width), lambda i: (i, 0))

    args = [x, mod, norm_g.reshape(1, D_MODEL), w_in, w_out]
    in_specs = [
        row_spec(D_MODEL),
        pl.BlockSpec((1, N_MOD, D_MODEL), lambda i: (i // tiles_per_mod, 0, 0)),
        _resident((1, D_MODEL)),
        _resident((D_MODEL, 2 * D_FF), layer),
        _resident((D_FF, D_MODEL), layer),
    ]
    out_shape = [jax.ShapeDtypeStruct((n, D_MODEL), F32)]
    out_specs = [row_spec(D_MODEL)]
    if pro == "outproj":
        args += [outproj[0], outproj[1]]
        in_specs += [row_spec(D_MODEL), _resident((D_MODEL, D_MODEL), layer)]
    scratch = [pltpu.VMEM((tm, D_FF), BF16)]
    if epi == "inproj":
        args += [inproj[0].reshape(1, D_MODEL), inproj[1], inproj[2].reshape(1, MLP_WIDTH),
                 inproj[3], inproj[4].reshape(1, LRU_WIDTH)]
        in_specs += [_resident((1, D_MODEL)), _resident((D_MODEL, IN_PROJ_WIDTH), layer),
                     _resident((1, MLP_WIDTH)), _resident((4, LRU_WIDTH)),
                     _resident((1, LRU_WIDTH))]
        out_shape += [jax.ShapeDtypeStruct((n, 4 * LRU_WIDTH), F32),
                      jax.ShapeDtypeStruct((n, MLP_WIDTH), BF16)]
        out_specs += [row_spec(4 * LRU_WIDTH), row_spec(MLP_WIDTH)]
        scratch += [pltpu.VMEM((tm + 2 * SUBLANES, LRU_WIDTH), F32)]
    elif epi == "final":
        args += [final_g.reshape(1, D_MODEL)]
        in_specs += [_resident((1, D_MODEL))]
    res = pl.pallas_call(
        functools.partial(_ffn_kernel, k0=k0, pro=pro, epi=epi),
        grid=(n // tm,),
        in_specs=in_specs,
        out_specs=out_specs,
        out_shape=out_shape,
        scratch_shapes=scratch,
        compiler_params=_cparams(1),
        name="ffn_" + (pro or "plain") + "_" + (epi or "plain"),
    )(*args)
    return res if epi == "inproj" else res[0]


def _short_conv(xe_ref, cw_ref, cb_ref, ts, halo):
    xe = xe_ref[...]
    rows = ts + 2 * halo
    xc = cb_ref[...] + xe[halo:halo + ts] * cw_ref[2:3, :]
    for k, sh in ((0, 2), (1, 1), (3, rows - 1)):
        xc = xc + pltpu.roll(xe, sh, 0)[halo:halo + ts] * cw_ref[k:k + 1, :]
    return xc


def _finish_conv(xc_ref, xi_refs, xp_ref, xn_ref, cw_ref, tile, nt, ts, r0, tc):
    zeros = jnp.zeros((SUBLANES, LRU_WIDTH), F32)
    r1 = r0 + tc
    before = (jnp.where(tile > 0, xp_ref[...], 0.0) if r0 == 0
              else xi_refs[2 * (r0 // tc - 1)][...])
    after = (jnp.where(tile < nt - 1, xn_ref[...], 0.0) if r1 == ts
             else xi_refs[2 * (r1 // tc - 1) + 1][...])
    e = jnp.concatenate([before, zeros], axis=0)
    head = (pltpu.roll(e, 2, 0)[SUBLANES:] * cw_ref[0:1, :]
            + pltpu.roll(e, 1, 0)[SUBLANES:] * cw_ref[1:2, :])
    f = jnp.concatenate([zeros, after], axis=0)
    tail = pltpu.roll(f, 2 * SUBLANES - 1, 0)[0:SUBLANES] * cw_ref[3:4, :]
    return jnp.concatenate([xc_ref[r0:r0 + SUBLANES] + head,
                            xc_ref[r0 + SUBLANES:r1 - SUBLANES],
                            xc_ref[r1 - SUBLANES:r1] + tail], axis=0)


def _lru_scan(d, xc, wg_ref, br_ref, bi_ref, lam_ref, h_ref, g0, carry, ts):
    groups = ts // SUBLANES
    sub = lax.broadcasted_iota(jnp.int32, (groups, SUBLANES, HALF), 1)
    order = range(groups) if d == 0 else range(groups - 1, -1, -1)
    row_id = lax.broadcasted_iota(jnp.int32, (SUBLANES, HALF), 0)
    near = row_id < SUBLANES // 2 if d == 0 else row_id >= SUBLANES // 2
    carry_out = []
    for hh in range(N_HALVES):
        cols = slice(hh * HALF, (hh + 1) * HALF)
        xch = xc[:, cols]
        kk = (-0.5 * RG_C) * _softplus(-lam_ref[d:d + 1, cols])
        pre = jnp.dot(xch.astype(BF16), wg_ref[d, hh], preferred_element_type=F32)
        tr = jnp.tanh(pre[:, :HALF] + 0.5 * br_ref[d:d + 1, cols])
        ti = jnp.tanh(pre[:, HALF:] + 0.5 * bi_ref[d:d + 1, cols])
        log_a = kk + kk * tr
        a = jnp.exp(log_a)
        t = jnp.tanh(log_a)
        q = t / (t - 1.0)
        gain = q * lax.rsqrt(jnp.maximum(q, TINY))
        bc = gain * ((0.7071067811865476 * xch) * (1.0 + ti))
        a3 = a.reshape(groups, SUBLANES, HALF)
        b3 = bc.reshape(groups, SUBLANES, HALF)
        for s in (1, 2):
            if d == 0:
                sh, m = s, sub >= s
            else:
                sh, m = SUBLANES - s, sub < SUBLANES - s
            am = jnp.where(m, a3, 0.0)
            b3 = b3 + am * pltpu.roll(b3, sh, 1)
            a3 = jnp.where(m, a3 * pltpu.roll(a3, sh, 1), a3)
        c = carry[:, cols]
        for g in order:
            h_near = a3[g] * c + b3[g]
            h_far = a3[g] * pltpu.roll(h_near, SUBLANES // 2, 0) + b3[g]
            h = jnp.where(near, h_near, h_far)
            h_ref[g0 + g, :, cols] = h
            row = h[SUBLANES - 1:SUBLANES, :] if d == 0 else h[0:1, :]
            c = jnp.broadcast_to(row, (SUBLANES, HALF))
        carry_out.append(c)
    return jnp.concatenate(carry_out, axis=1)


def _lru_bwd_kernel(*refs, ts, nt, tc):
    n_int = 2 * (ts // tc - 1)
    xc_ref, xp_ref, xn_ref = refs[:3]
    xi_refs = refs[3:3 + n_int]
    cw_ref, wg_ref, br_ref, bi_ref, lam_ref, h0_ref, hb_ref, carry_ref = refs[3 + n_int:]
    j = pl.program_id(1)

    @pl.when(j == 0)
    def _():
        carry_ref[...] = jnp.broadcast_to(h0_ref[0], (SUBLANES, LRU_WIDTH))

    carry = carry_ref[...]
    for r0 in range(ts - tc, -1, -tc):
        xc = _finish_conv(xc_ref, xi_refs, xp_ref, xn_ref, cw_ref, nt - 1 - j, nt, ts, r0, tc)
        carry = _lru_scan(1, xc, wg_ref, br_ref, bi_ref, lam_ref, hb_ref,
                          r0 // SUBLANES, carry, tc)
    carry_ref[...] = carry


def _mix_fwd_kernel(*refs, ts, nt, tc):
    n_int = 2 * (ts // tc - 1)
    xc_ref, xp_ref, xn_ref = refs[:3]
    xi_refs = refs[3:3 + n_int]
    (gg_ref, u_ref, vn_ref, hb_ref, cw_ref, wg_ref, br_ref, bi_ref, lam_ref, h0_ref, sw_ref,
     sb_ref, y_ref, hlast_ref, hf_ref, carry_ref) = refs[3 + n_int:]
    j = pl.program_id(1)
    n_chunks = ts // CHUNK

    @pl.when(j == 0)
    def _():
        carry_ref[...] = jnp.broadcast_to(h0_ref[0], (SUBLANES, LRU_WIDTH))

    carry = carry_ref[...]
    for r0 in range(0, ts, tc):
        xc = _finish_conv(xc_ref, xi_refs, xp_ref, xn_ref, cw_ref, j, nt, ts, r0, tc)
        carry = _lru_scan(0, xc, wg_ref, br_ref, bi_ref, lam_ref, hf_ref,
                          r0 // SUBLANES, carry, tc)
    carry_ref[...] = carry
    hlast_ref[0] = carry
    for g in range(MLP_GROUPS):
        cols = slice(g * MLP_GROUP_DIM, (g + 1) * MLP_GROUP_DIM)
        rhs = jnp.concatenate(
            [vn_ref[ch * CHUNK:(ch + 1) * CHUNK, cols] for ch in range(n_chunks)], axis=1)
        z = jnp.dot(sw_ref[g], rhs, preferred_element_type=F32)
        for ch in range(n_chunks):
            rows = slice(ch * CHUNK, (ch + 1) * CHUNK)
            zc = z[:, ch * MLP_GROUP_DIM:(ch + 1) * MLP_GROUP_DIM] + sb_ref[:, cols]
            y_ref[rows, LRU_WIDTH + g * MLP_GROUP_DIM:LRU_WIDTH + (g + 1) * MLP_GROUP_DIM] = (
                u_ref[rows, cols] * zc).astype(BF16)

    h_sum = hf_ref[...].reshape(ts, LRU_WIDTH) + hb_ref[...]
    y_ref[:, 0:LRU_WIDTH] = (h_sum * gg_ref[...]).astype(BF16)


def _window_specs(seq, ts, tc, reverse):
    nt = seq // ts
    rb = ts // SUBLANES
    sb = seq // SUBLANES

    def tidx(j):
        return nt - 1 - j if reverse else j

    def rows8(offset):
        return pl.BlockSpec((SUBLANES, LRU_WIDTH),
                            lambda b, j: (b * sb + tidx(j) * rb + offset, 0))

    specs = [
        pl.BlockSpec((ts, LRU_WIDTH), lambda b, j: (b * nt + tidx(j), 1)),
        pl.BlockSpec((SUBLANES, LRU_WIDTH),
                     lambda b, j: (b * sb + jnp.maximum(tidx(j) * rb - 1, 0), 0)),
        pl.BlockSpec((SUBLANES, LRU_WIDTH),
                     lambda b, j: (b * sb + jnp.minimum((tidx(j) + 1) * rb, sb - 1), 0)),
    ]
    for r in range(tc, ts, tc):
        specs += [rows8(r // SUBLANES - 1), rows8(r // SUBLANES)]
    return specs


def _mix(p, vn, seq, lru, h0_f, h0_b, sgu_w, sgu_bias, *, ts, tc, ts_bwd=None):
    conv_w, wg, b_r, b_i, lam = lru
    n = p.shape[0]
    bsz = n // seq
    nt = seq // ts
    groups = ts // SUBLANES
    state_spec = pl.BlockSpec((1, 1, LRU_WIDTH), lambda b, j: (b, 0, 0))
    lru_specs = [
        _resident((4, LRU_WIDTH)),
        _resident((2, N_HALVES, HALF, 2 * HALF)),
        _resident((2, LRU_WIDTH)),
        _resident((2, LRU_WIDTH)),
        _resident((2, LRU_WIDTH)),
    ]
    lru_args = (conv_w, wg, b_r, b_i, lam)
    carry_scratch = pltpu.VMEM((SUBLANES, LRU_WIDTH), F32)

    def tile_spec(width, col=0):
        return pl.BlockSpec((ts, width), lambda b, j: (b * nt + j, col))

    tsb = ts_bwd or ts
    ntb = seq // tsb
    hb = pl.pallas_call(
        functools.partial(_lru_bwd_kernel, ts=tsb, nt=ntb, tc=tc),
        grid=(bsz, ntb),
        in_specs=_window_specs(seq, tsb, tc, True) + lru_specs + [state_spec],
        out_specs=pl.BlockSpec((tsb // SUBLANES, SUBLANES, LRU_WIDTH),
                               lambda b, j: (b * ntb + ntb - 1 - j, 0, 0)),
        out_shape=jax.ShapeDtypeStruct((n // SUBLANES, SUBLANES, LRU_WIDTH), F32),
        scratch_shapes=[carry_scratch],
        compiler_params=_cparams(2),
        name="lru_bwd",
    )(p, *[p] * (2 * (tsb // tc)), *lru_args, h0_b)
    hb = hb.reshape(n, LRU_WIDTH)

    y, h_last = pl.pallas_call(
        functools.partial(_mix_fwd_kernel, ts=ts, nt=nt, tc=tc),
        grid=(bsz, nt),
        in_specs=_window_specs(seq, ts, tc, False) + [
            tile_spec(LRU_WIDTH, 2), tile_spec(MLP_WIDTH, 3), tile_spec(MLP_WIDTH),
            tile_spec(LRU_WIDTH),
        ] + lru_specs + [
            state_spec,
            _resident((MLP_GROUPS, CHUNK, CHUNK)),
            _resident((CHUNK, MLP_WIDTH)),
        ],
        out_specs=[
            tile_spec(D_MODEL),
            pl.BlockSpec((1, SUBLANES, LRU_WIDTH), lambda b, j: (b, 0, 0)),
        ],
        out_shape=[
            jax.ShapeDtypeStruct((n, D_MODEL), BF16),
            jax.ShapeDtypeStruct((bsz, SUBLANES, LRU_WIDTH), F32),
        ],
        scratch_shapes=[
            pltpu.VMEM((groups, SUBLANES, LRU_WIDTH), F32),
            carry_scratch,
        ],
        compiler_params=_cparams(2),
        name="mix_fwd",
    )(p, *[p] * (2 * (ts // tc)), p, p, vn, hb, *lru_args, h0_f, sgu_w, sgu_bias)
    return y, h_last[:, 0:1, :], hb


def _gate_weights(w_r, w_i):
    hpb = HALF // LRU_HEAD_DIM
    out = []
    for d in range(2):
        halves = []
        for hh in range(N_HALVES):
            blocks_r = [w_r[d, hh * hpb + k] for k in range(hpb)]
            blocks_i = [w_i[d, hh * hpb + k] for k in range(hpb)]
            halves.append(jnp.concatenate(
                [jax.scipy.linalg.block_diag(*blocks_r), jax.scipy.linalg.block_diag(*blocks_i)],
                axis=1))
        out.append(jnp.stack(halves))
    return (0.5 * jnp.stack(out)).astype(BF16)


def kernel(x, c, ctx, c_ctx, w_ada, b_ada, ffn1_norm_g, ffn1_w_in, ffn1_w_out,
           mix_norm_g, w_in_mix, lru_conv_w, lru_conv_b, lru_w_r, lru_b_r, lru_w_i,
           lru_b_i, lru_lambda, sgu_norm_g, sgu_w, sgu_b, w_out_mix,
           ffn2_norm_g, ffn2_w_in, ffn2_w_out, final_norm_g):
    bsz, n_lat, d = x.shape
    n_ctx = ctx.shape[1]
    depth = w_ada.shape[0]
    assert bsz + 1 <= MOD_ROWS and d == D_MODEL
    tm_lat, tm_ctx = 512, n_ctx

    c_rows = jnp.zeros((MOD_ROWS, d), F32).at[:bsz].set(c).at[bsz].set(c_ctx)
    mods = _ada(c_rows, w_ada, b_ada).reshape(depth, MOD_ROWS, N_MOD, d)

    h = x.reshape(bsz * n_lat, d)
    hc = ctx.reshape(bsz * n_ctx, d)
    zeros_state = jnp.zeros((bsz, 1, LRU_WIDTH), F32)
    w1_in, w1_out = _to_bf16(ffn1_w_in), _to_bf16(ffn1_w_out)
    w2_in, w2_out = _to_bf16(ffn2_w_in), _to_bf16(ffn2_w_out)
    w_mix_in, w_mix_out = _to_bf16(w_in_mix), _to_bf16(w_out_mix)

    for l in range(depth):
        last = l == depth - 1
        m = mods[l, :bsz]
        mc = mods[l, bsz:bsz + 1]
        lru = (lru_conv_w[l], _gate_weights(lru_w_r[l], lru_w_i[l]),
               lru_b_r[l], lru_b_i[l], lru_lambda[l])
        s_w = sgu_w[l].astype(BF16)
        s_bias = jnp.repeat(sgu_b[l].T, MLP_GROUP_DIM, axis=1)
        inproj = (mix_norm_g[l], w_mix_in, sgu_norm_g[l], lru_conv_w[l], lru_conv_b[l])

        hc, p, vn = _ffn(hc, mc, 0, ffn1_norm_g[l], w1_in, w1_out, l, tm=tm_ctx, inproj=inproj)
        yc, hf_last, hb_c = _mix(p, vn, n_ctx, lru, zeros_state, zeros_state,
                                 s_w, s_bias, ts=n_ctx, tc=tm_ctx)
        h0_b = hb_c.reshape(bsz, n_ctx, LRU_WIDTH)[:, 0:1, :]
        if not last:
            hc = _ffn(hc, mc, 6, ffn2_norm_g[l], w2_in, w2_out, l, tm=2 * tm_ctx,
                      outproj=(yc, w_mix_out))

        h, p, vn = _ffn(h, m, 0, ffn1_norm_g[l], w1_in, w1_out, l, tm=tm_lat, inproj=inproj)
        y, _, _ = _mix(p, vn, n_lat, lru, hf_last, h0_b, s_w, s_bias,
                       ts=2 * tm_lat, tc=tm_lat, ts_bwd=4 * tm_lat)
        h = _ffn(h, m, 6, ffn2_norm_g[l], w2_in, w2_out, l, tm=2 * tm_lat,
                 outproj=(y, w_mix_out), final_g=final_norm_g if last else None)

    return h.reshape(bsz, n_lat, d)
```

```python
import functools

import jax
import jax.numpy as jnp
from jax import lax
from jax.experimental import pallas as pl
from jax.experimental.pallas import tpu as pltpu

F32 = jnp.float32
BF16 = jnp.bfloat16

EPS = 1e-6
D_MODEL = 1024
D_FF = 2816
LRU_WIDTH = 512
LRU_HEADS = 8
LRU_HEAD_DIM = 64
RG_C = 8.0
MLP_GROUPS = 4
MLP_GROUP_DIM = 128
MLP_WIDTH = 512
CHUNK = 128
IN_PROJ_WIDTH = 2048
N_MOD = 9
MOD_ROWS = 8
SUBLANES = 8
HALF = 256
N_HALVES = LRU_WIDTH // HALF
TINY = 1e-30
FFN_CHUNK = 256
CAST_BLOCK_ELEMS = 2 * 1024 * 1024

VMEM_LIMIT = 56 * 1024 * 1024


def _cparams(n_axes):
    return pltpu.CompilerParams(
        dimension_semantics=("arbitrary",) * n_axes,
        vmem_limit_bytes=VMEM_LIMIT,
    )


def _resident(shape, layer=None):
    nd = len(shape)
    if layer is None:
        return pl.BlockSpec(shape, lambda *_: (0,) * nd, pipeline_mode=pl.Buffered(1))
    return pl.BlockSpec((None,) + tuple(shape), lambda *_: (layer,) + (0,) * nd,
                        pipeline_mode=pl.Buffered(1))


def _cast_kernel(x_ref, o_ref):
    o_ref[...] = x_ref[...].astype(BF16)


def _to_bf16(w):
    depth, rows, cols = w.shape
    br = max(b for b in range(16, rows + 1, 16)
             if rows % b == 0 and b * cols <= CAST_BLOCK_ELEMS)
    spec = pl.BlockSpec((1, br, cols), lambda l, i: (l, i, 0))
    return pl.pallas_call(
        _cast_kernel,
        grid=(depth, rows // br),
        in_specs=[spec],
        out_specs=spec,
        out_shape=jax.ShapeDtypeStruct(w.shape, BF16),
        compiler_params=_cparams(2),
        name="to_bf16",
    )(w)


def _sigmoid(x):
    return 1.0 / (1.0 + jnp.exp(-x))


def _softplus(x):
    return jnp.maximum(x, 0.0) + jnp.log1p(jnp.exp(-jnp.abs(x)))


def _gelu_tanh(x):
    return 0.5 * x * (1.0 + jnp.tanh(0.7978845608028654 * (x + 0.044715 * (x * x * x))))


def _norm_mod(x, g, shift, scale):
    ms = jnp.mean(x * x, axis=-1, keepdims=True)
    return (x * lax.rsqrt(ms + EPS)) * (g * (1.0 + scale)) + shift


def _ada_kernel(c_ref, w_ref, b_ref, o_ref):
    c = c_ref[...]
    sc = c * _sigmoid(c)
    w = w_ref[0]
    s_hi = sc.astype(BF16)
    s_lo = (sc - s_hi.astype(F32)).astype(BF16)
    w_hi = w.astype(BF16)
    w_lo = (w - w_hi.astype(F32)).astype(BF16)
    p = jnp.dot(jnp.concatenate([s_hi, s_lo], axis=0), w_hi, preferred_element_type=F32)
    p = p[:MOD_ROWS] + p[MOD_ROWS:] + jnp.dot(s_hi, w_lo, preferred_element_type=F32)
    o_ref[0] = p + b_ref[0]


def _ada(c_rows, w_ada, b_ada):
    depth = w_ada.shape[0]
    return pl.pallas_call(
        _ada_kernel,
        grid=(depth, N_MOD),
        in_specs=[
            pl.BlockSpec((MOD_ROWS, D_MODEL), lambda l, j: (0, 0)),
            pl.BlockSpec((1, D_MODEL, D_MODEL), lambda l, j: (l, 0, j)),
            pl.BlockSpec((1, 1, D_MODEL), lambda l, j: (l, 0, j)),
        ],
        out_specs=pl.BlockSpec((1, MOD_ROWS, D_MODEL), lambda l, j: (l, 0, j)),
        out_shape=jax.ShapeDtypeStruct((depth, MOD_ROWS, N_MOD * D_MODEL), F32),
        compiler_params=_cparams(2),
        name="ada_mod",
    )(c_rows, w_ada, b_ada.reshape(depth, 1, N_MOD * D_MODEL))


def _ffn_kernel(*refs, k0, pro, epi):
    refs = list(refs)
    x_ref, mod_ref, g_ref, win_ref, wout_ref = refs[:5]
    pos = 5
    if pro == "outproj":
        y_ref, wo_ref = refs[pos:pos + 2]
        pos += 2
    if epi == "inproj":
        gm_ref, wmix_ref, sg_ref, cw_ref, cb_ref = refs[pos:pos + 5]
        pos += 5
    elif epi == "final":
        fg_ref = refs[pos]
        pos += 1
    o_ref = refs[pos]
    pos += 1
    if epi == "inproj":
        xl_ref, xc_ref, gg_ref, u_ref, vn_ref = refs[pos:pos + 5]
        pos += 5
    act_ref = refs[pos]
    pos += 1
    if epi == "inproj":
        xe_ref = refs[pos]
        tm = x_ref.shape[0]

        @pl.when(pl.program_id(0) == 0)
        def _():
            xe_ref[...] = jnp.zeros_like(xe_ref)

    shift = mod_ref[0, k0:k0 + 1, :]
    scale = mod_ref[0, k0 + 1:k0 + 2, :]
    gate = mod_ref[0, k0 + 2:k0 + 3, :]
    if pro == "outproj":
        hm = x_ref.shape[0] // 2
        hs, zs = [], []
        for r in (slice(0, hm), slice(hm, 2 * hm)):
            hr = x_ref[r, :] + mod_ref[0, 5:6, :] * jnp.dot(y_ref[r, :], wo_ref[...],
                                                            preferred_element_type=F32)
            hs.append(hr)
            zs.append(_norm_mod(hr, g_ref[...], shift, scale).astype(BF16))
        h = jnp.concatenate(hs, axis=0)
        zb = jnp.concatenate(zs, axis=0)
    else:
        h = x_ref[...]
        zb = _norm_mod(h, g_ref[...], shift, scale).astype(BF16)
    for c0 in range(0, D_FF, FFN_CHUNK):
        cw = min(FFN_CHUNK, D_FF - c0)
        g = jnp.dot(zb, win_ref[:, c0:c0 + cw], preferred_element_type=F32)
        u = jnp.dot(zb, win_ref[:, D_FF + c0:D_FF + c0 + cw], preferred_element_type=F32)
        act_ref[:, c0:c0 + cw] = (g * _sigmoid(g) * u).astype(BF16)
    out = h + (0.5 * gate) * jnp.dot(act_ref[...], wout_ref[...], preferred_element_type=F32)
    if epi == "final":
        ms = jnp.mean(out * out, axis=-1, keepdims=True)
        out = out * lax.rsqrt(ms + EPS) * fg_ref[...]
    o_ref[...] = out
    if epi == "inproj":
        zm = _norm_mod(out, gm_ref[...], mod_ref[0, 3:4, :], mod_ref[0, 4:5, :]).astype(BF16)
        v = jnp.dot(zm, wmix_ref[:, 2 * LRU_WIDTH + MLP_WIDTH:], preferred_element_type=F32)
        vc = v - jnp.mean(v, axis=-1, keepdims=True)
        var = jnp.mean(vc * vc, axis=-1, keepdims=True)
        vn_ref[...] = (vc * lax.rsqrt(var + EPS) * sg_ref[...]).astype(BF16)
        xl = jnp.dot(zm, wmix_ref[:, 0:LRU_WIDTH], preferred_element_type=F32)
        xl_ref[...] = xl
        xe_ref[SUBLANES:SUBLANES + tm] = xl
        xc_ref[...] = _short_conv(xe_ref, cw_ref, cb_ref, tm, SUBLANES)
        gl = jnp.dot(zm, wmix_ref[:, LRU_WIDTH:2 * LRU_WIDTH], preferred_element_type=F32)
        gg_ref[...] = _gelu_tanh(gl).astype(gg_ref.dtype)
        u_ref[...] = jnp.dot(zm, wmix_ref[:, 2 * LRU_WIDTH:2 * LRU_WIDTH + MLP_WIDTH],
                             preferred_element_type=F32).astype(u_ref.dtype)


def _ffn(x, mod, k0, norm_g, w_in, w_out, layer, *, tm, outproj=None, inproj=None,
         final_g=None):
    n = x.shape[0]
    tiles_per_mod = n // mod.shape[0] // tm
    pro = "outproj" if outproj is not None else None
    epi = "inproj" if inproj is not None else ("final" if final_g is not None else None)

    def row_spec(width):
        return pl.BlockSpec((tm, width), lambda i: (i, 0))

    args = [x, mod, norm_g.reshape(1, D_MODEL), w_in, w_out]
    in_specs = [
        row_spec(D_MODEL),
        pl.BlockSpec((1, N_MOD, D_MODEL), lambda i: (i // tiles_per_mod, 0, 0)),
        _resident((1, D_MODEL)),
        _resident((D_MODEL, 2 * D_FF), layer),
        _resident((D_FF, D_MODEL), layer),
    ]
    out_shape = [jax.ShapeDtypeStruct((n, D_MODEL), F32)]
    out_specs = [row_spec(D_MODEL)]
    if pro == "outproj":
        args += [outproj[0], outproj[1]]
        in_specs += [row_spec(D_MODEL), _resident((D_MODEL, D_MODEL), layer)]
    scratch = [pltpu.VMEM((tm, D_FF), BF16)]
    if epi == "inproj":
        args += [inproj[0].reshape(1, D_MODEL), inproj[1], inproj[2].reshape(1, MLP_WIDTH),
                 inproj[3], inproj[4].reshape(1, LRU_WIDTH)]
        in_specs += [_resident((1, D_MODEL)), _resident((D_MODEL, IN_PROJ_WIDTH), layer),
                     _resident((1, MLP_WIDTH)), _resident((4, LRU_WIDTH)),
                     _resident((1, LRU_WIDTH))]
        out_shape += [jax.ShapeDtypeStruct((n, LRU_WIDTH), F32),
                      jax.ShapeDtypeStruct((n, LRU_WIDTH), F32),
                      jax.ShapeDtypeStruct((n, LRU_WIDTH), BF16),
                      jax.ShapeDtypeStruct((n, MLP_WIDTH), BF16),
                      jax.ShapeDtypeStruct((n, MLP_WIDTH), BF16)]
        out_specs += [row_spec(LRU_WIDTH), row_spec(LRU_WIDTH), row_spec(LRU_WIDTH),
                      row_spec(MLP_WIDTH), row_spec(MLP_WIDTH)]
        scratch += [pltpu.VMEM((tm + 2 * SUBLANES, LRU_WIDTH), F32)]
    elif epi == "final":
        args += [final_g.reshape(1, D_MODEL)]
        in_specs += [_resident((1, D_MODEL))]
    res = pl.pallas_call(
        functools.partial(_ffn_kernel, k0=k0, pro=pro, epi=epi),
        grid=(n // tm,),
        in_specs=in_specs,
        out_specs=out_specs,
        out_shape=out_shape,
        scratch_shapes=scratch,
        compiler_params=_cparams(1),
        name="ffn_" + (pro or "plain") + "_" + (epi or "plain"),
    )(*args)
    return res if epi == "inproj" else res[0]


def _short_conv(xe_ref, cw_ref, cb_ref, ts, halo):
    xe = xe_ref[...]
    rows = ts + 2 * halo
    xc = cb_ref[...] + xe[halo:halo + ts] * cw_ref[2:3, :]
    for k, sh in ((0, 2), (1, 1), (3, rows - 1)):
        xc = xc + pltpu.roll(xe, sh, 0)[halo:halo + ts] * cw_ref[k:k + 1, :]
    return xc


def _finish_conv(xc_ref, xi_refs, xp_ref, xn_ref, cw_ref, tile, nt, ts, r0, tc):
    zeros = jnp.zeros((SUBLANES, LRU_WIDTH), F32)
    r1 = r0 + tc
    before = (jnp.where(tile > 0, xp_ref[...], 0.0) if r0 == 0
              else xi_refs[2 * (r0 // tc - 1)][...])
    after = (jnp.where(tile < nt - 1, xn_ref[...], 0.0) if r1 == ts
             else xi_refs[2 * (r1 // tc - 1) + 1][...])
    e = jnp.concatenate([before, zeros], axis=0)
    head = (pltpu.roll(e, 2, 0)[SUBLANES:] * cw_ref[0:1, :]
            + pltpu.roll(e, 1, 0)[SUBLANES:] * cw_ref[1:2, :])
    f = jnp.concatenate([zeros, after], axis=0)
    tail = pltpu.roll(f, 2 * SUBLANES - 1, 0)[0:SUBLANES] * cw_ref[3:4, :]
    return jnp.concatenate([xc_ref[r0:r0 + SUBLANES] + head,
                            xc_ref[r0 + SUBLANES:r1 - SUBLANES],
                            xc_ref[r1 - SUBLANES:r1] + tail], axis=0)


def _lru_scan(d, xc, wg_ref, br_ref, bi_ref, lam_ref, h_ref, g0, carry, ts):
    groups = ts // SUBLANES
    sub = lax.broadcasted_iota(jnp.int32, (groups, SUBLANES, HALF), 1)
    order = range(groups) if d == 0 else range(groups - 1, -1, -1)
    row_id = lax.broadcasted_iota(jnp.int32, (SUBLANES, HALF), 0)
    near = row_id < SUBLANES // 2 if d == 0 else row_id >= SUBLANES // 2
    carry_out = []
    for hh in range(N_HALVES):
        cols = slice(hh * HALF, (hh + 1) * HALF)
        xch = xc[:, cols]
        kk = (-0.5 * RG_C) * _softplus(-lam_ref[d:d + 1, cols])
        pre = jnp.dot(xch.astype(BF16), wg_ref[d, hh], preferred_element_type=F32)
        tr = jnp.tanh(pre[:, :HALF] + 0.5 * br_ref[d:d + 1, cols])
        ti = jnp.tanh(pre[:, HALF:] + 0.5 * bi_ref[d:d + 1, cols])
        log_a = kk + kk * tr
        a = jnp.exp(log_a)
        t = jnp.tanh(log_a)
        q = t / (t - 1.0)
        gain = q * lax.rsqrt(jnp.maximum(q, TINY))
        bc = gain * ((0.7071067811865476 * xch) * (1.0 + ti))
        a3 = a.reshape(groups, SUBLANES, HALF)
        b3 = bc.reshape(groups, SUBLANES, HALF)
        for s in (1, 2):
            if d == 0:
                sh, m = s, sub >= s
            else:
                sh, m = SUBLANES - s, sub < SUBLANES - s
            am = jnp.where(m, a3, 0.0)
            b3 = b3 + am * pltpu.roll(b3, sh, 1)
            a3 = jnp.where(m, a3 * pltpu.roll(a3, sh, 1), a3)
        c = carry[:, cols]
        for g in order:
            h_near = a3[g] * c + b3[g]
            h_far = a3[g] * pltpu.roll(h_near, SUBLANES // 2, 0) + b3[g]
            h = jnp.where(near, h_near, h_far)
            h_ref[g0 + g, :, cols] = h
            row = h[SUBLANES - 1:SUBLANES, :] if d == 0 else h[0:1, :]
            c = jnp.broadcast_to(row, (SUBLANES, HALF))
        carry_out.append(c)
    return jnp.concatenate(carry_out, axis=1)


def _lru_bwd_kernel(*refs, ts, nt, tc):
    n_int = 2 * (ts // tc - 1)
    xc_ref, xp_ref, xn_ref = refs[:3]
    xi_refs = refs[3:3 + n_int]
    cw_ref, wg_ref, br_ref, bi_ref, lam_ref, h0_ref, hb_ref, carry_ref = refs[3 + n_int:]
    j = pl.program_id(1)

    @pl.when(j == 0)
    def _():
        carry_ref[...] = jnp.broadcast_to(h0_ref[0], (SUBLANES, LRU_WIDTH))

    carry = carry_ref[...]
    for r0 in range(ts - tc, -1, -tc):
        xc = _finish_conv(xc_ref, xi_refs, xp_ref, xn_ref, cw_ref, nt - 1 - j, nt, ts, r0, tc)
        carry = _lru_scan(1, xc, wg_ref, br_ref, bi_ref, lam_ref, hb_ref,
                          r0 // SUBLANES, carry, tc)
    carry_ref[...] = carry


def _mix_fwd_kernel(*refs, ts, nt, tc):
    n_int = 2 * (ts // tc - 1)
    xc_ref, xp_ref, xn_ref = refs[:3]
    xi_refs = refs[3:3 + n_int]
    (gg_ref, u_ref, vn_ref, hb_ref, cw_ref, wg_ref, br_ref, bi_ref, lam_ref, h0_ref, sw_ref,
     sb_ref, y_ref, hlast_ref, hf_ref, carry_ref) = refs[3 + n_int:]
    j = pl.program_id(1)
    n_chunks = ts // CHUNK

    @pl.when(j == 0)
    def _():
        carry_ref[...] = jnp.broadcast_to(h0_ref[0], (SUBLANES, LRU_WIDTH))

    carry = carry_ref[...]
    for r0 in range(0, ts, tc):
        xc = _finish_conv(xc_ref, xi_refs, xp_ref, xn_ref, cw_ref, j, nt, ts, r0, tc)
        carry = _lru_scan(0, xc, wg_ref, br_ref, bi_ref, lam_ref, hf_ref,
                          r0 // SUBLANES, carry, tc)
    carry_ref[...] = carry
    hlast_ref[0] = carry
    for g in range(MLP_GROUPS):
        cols = slice(g * MLP_GROUP_DIM, (g + 1) * MLP_GROUP_DIM)
        rhs = jnp.concatenate(
            [vn_ref[ch * CHUNK:(ch + 1) * CHUNK, cols] for ch in range(n_chunks)], axis=1)
        z = jnp.dot(sw_ref[g], rhs, preferred_element_type=F32)
        for ch in range(n_chunks):
            rows = slice(ch * CHUNK, (ch + 1) * CHUNK)
            zc = z[:, ch * MLP_GROUP_DIM:(ch + 1) * MLP_GROUP_DIM] + sb_ref[:, cols]
            y_ref[rows, LRU_WIDTH + g * MLP_GROUP_DIM:LRU_WIDTH + (g + 1) * MLP_GROUP_DIM] = (
                u_ref[rows, cols] * zc).astype(BF16)

    h_sum = hf_ref[...].reshape(ts, LRU_WIDTH) + hb_ref[...]
    y_ref[:, 0:LRU_WIDTH] = (h_sum * gg_ref[...]).astype(BF16)


def _window_specs(seq, ts, tc, reverse):
    nt = seq // ts
    rb = ts // SUBLANES
    sb = seq // SUBLANES

    def tidx(j):
        return nt - 1 - j if reverse else j

    def rows8(offset):
        return pl.BlockSpec((SUBLANES, LRU_WIDTH),
                            lambda b, j: (b * sb + tidx(j) * rb + offset, 0))

    specs = [
        pl.BlockSpec((ts, LRU_WIDTH), lambda b, j: (b * nt + tidx(j), 0)),
        pl.BlockSpec((SUBLANES, LRU_WIDTH),
                     lambda b, j: (b * sb + jnp.maximum(tidx(j) * rb - 1, 0), 0)),
        pl.BlockSpec((SUBLANES, LRU_WIDTH),
                     lambda b, j: (b * sb + jnp.minimum((tidx(j) + 1) * rb, sb - 1), 0)),
    ]
    for r in range(tc, ts, tc):
        specs += [rows8(r // SUBLANES - 1), rows8(r // SUBLANES)]
    return specs


def _mix(xl, xc, gg, u, vn, seq, lru, h0_f, h0_b, sgu_w, sgu_bias, *, ts, tc, ts_bwd=None):
    conv_w, wg, b_r, b_i, lam = lru
    n = xl.shape[0]
    bsz = n // seq
    nt = seq // ts
    groups = ts // SUBLANES
    state_spec = pl.BlockSpec((1, 1, LRU_WIDTH), lambda b, j: (b, 0, 0))
    lru_specs = [
        _resident((4, LRU_WIDTH)),
        _resident((2, N_HALVES, HALF, 2 * HALF)),
        _resident((2, LRU_WIDTH)),
        _resident((2, LRU_WIDTH)),
        _resident((2, LRU_WIDTH)),
    ]
    lru_args = (conv_w, wg, b_r, b_i, lam)
    carry_scratch = pltpu.VMEM((SUBLANES, LRU_WIDTH), F32)

    def tile_spec(width):
        return pl.BlockSpec((ts, width), lambda b, j: (b * nt + j, 0))

    tsb = ts_bwd or ts
    ntb = seq // tsb
    hb = pl.pallas_call(
        functools.partial(_lru_bwd_kernel, ts=tsb, nt=ntb, tc=tc),
        grid=(bsz, ntb),
        in_specs=_window_specs(seq, tsb, tc, True) + lru_specs + [state_spec],
        out_specs=pl.BlockSpec((tsb // SUBLANES, SUBLANES, LRU_WIDTH),
                               lambda b, j: (b * ntb + ntb - 1 - j, 0, 0)),
        out_shape=jax.ShapeDtypeStruct((n // SUBLANES, SUBLANES, LRU_WIDTH), F32),
        scratch_shapes=[carry_scratch],
        compiler_params=_cparams(2),
        name="lru_bwd",
    )(xc, *[xl] * (2 * (tsb // tc)), *lru_args, h0_b)
    hb = hb.reshape(n, LRU_WIDTH)

    y, h_last = pl.pallas_call(
        functools.partial(_mix_fwd_kernel, ts=ts, nt=nt, tc=tc),
        grid=(bsz, nt),
        in_specs=_window_specs(seq, ts, tc, False) + [
            tile_spec(LRU_WIDTH), tile_spec(MLP_WIDTH), tile_spec(MLP_WIDTH), tile_spec(LRU_WIDTH),
        ] + lru_specs + [
            state_spec,
            _resident((MLP_GROUPS, CHUNK, CHUNK)),
            _resident((CHUNK, MLP_WIDTH)),
        ],
        out_specs=[
            tile_spec(D_MODEL),
            pl.BlockSpec((1, SUBLANES, LRU_WIDTH), lambda b, j: (b, 0, 0)),
        ],
        out_shape=[
            jax.ShapeDtypeStruct((n, D_MODEL), BF16),
            jax.ShapeDtypeStruct((bsz, SUBLANES, LRU_WIDTH), F32),
        ],
        scratch_shapes=[
            pltpu.VMEM((groups, SUBLANES, LRU_WIDTH), F32),
            carry_scratch,
        ],
        compiler_params=_cparams(2),
        name="mix_fwd",
    )(xc, *[xl] * (2 * (ts // tc)), gg, u, vn, hb, *lru_args, h0_f, sgu_w, sgu_bias)
    return y, h_last[:, 0:1, :], hb


def _gate_weights(w_r, w_i):
    hpb = HALF // LRU_HEAD_DIM
    out = []
    for d in range(2):
        halves = []
        for hh in range(N_HALVES):
            blocks_r = [w_r[d, hh * hpb + k] for k in range(hpb)]
            blocks_i = [w_i[d, hh * hpb + k] for k in range(hpb)]
            halves.append(jnp.concatenate(
                [jax.scipy.linalg.block_diag(*blocks_r), jax.scipy.linalg.block_diag(*blocks_i)],
                axis=1))
        out.append(jnp.stack(halves))
    return (0.5 * jnp.stack(out)).astype(BF16)


def kernel(x, c, ctx, c_ctx, w_ada, b_ada, ffn1_norm_g, ffn1_w_in, ffn1_w_out,
           mix_norm_g, w_in_mix, lru_conv_w, lru_conv_b, lru_w_r, lru_b_r, lru_w_i,
           lru_b_i, lru_lambda, sgu_norm_g, sgu_w, sgu_b, w_out_mix,
           ffn2_norm_g, ffn2_w_in, ffn2_w_out, final_norm_g):
    bsz, n_lat, d = x.shape
    n_ctx = ctx.shape[1]
    depth = w_ada.shape[0]
    assert bsz + 1 <= MOD_ROWS and d == D_MODEL
    tm_lat, tm_ctx = 512, n_ctx

    c_rows = jnp.zeros((MOD_ROWS, d), F32).at[:bsz].set(c).at[bsz].set(c_ctx)
    mods = _ada(c_rows, w_ada, b_ada).reshape(depth, MOD_ROWS, N_MOD, d)

    h = x.reshape(bsz * n_lat, d)
    hc = ctx.reshape(bsz * n_ctx, d)
    zeros_state = jnp.zeros((bsz, 1, LRU_WIDTH), F32)
    w1_in, w1_out = _to_bf16(ffn1_w_in), _to_bf16(ffn1_w_out)
    w2_in, w2_out = _to_bf16(ffn2_w_in), _to_bf16(ffn2_w_out)
    w_mix_in, w_mix_out = _to_bf16(w_in_mix), _to_bf16(w_out_mix)

    for l in range(depth):
        last = l == depth - 1
        m = mods[l, :bsz]
        mc = mods[l, bsz:bsz + 1]
        lru = (lru_conv_w[l], _gate_weights(lru_w_r[l], lru_w_i[l]),
               lru_b_r[l], lru_b_i[l], lru_lambda[l])
        s_w = sgu_w[l].astype(BF16)
        s_bias = jnp.repeat(sgu_b[l].T, MLP_GROUP_DIM, axis=1)
        inproj = (mix_norm_g[l], w_mix_in, sgu_norm_g[l], lru_conv_w[l], lru_conv_b[l])

        hc, xl, xc, gg, u, vn = _ffn(hc, mc, 0, ffn1_norm_g[l], w1_in, w1_out, l, tm=tm_ctx,
                                 inproj=inproj)
        yc, hf_last, hb_c = _mix(xl, xc, gg, u, vn, n_ctx, lru, zeros_state, zeros_state,
                                 s_w, s_bias, ts=n_ctx, tc=tm_ctx)
        h0_b = hb_c.reshape(bsz, n_ctx, LRU_WIDTH)[:, 0:1, :]
        if not last:
            hc = _ffn(hc, mc, 6, ffn2_norm_g[l], w2_in, w2_out, l, tm=2 * tm_ctx,
                      outproj=(yc, w_mix_out))

        h, xl, xc, gg, u, vn = _ffn(h, m, 0, ffn1_norm_g[l], w1_in, w1_out, l, tm=tm_lat,
                                inproj=inproj)
        y, _, _ = _mix(xl, xc, gg, u, vn, n_lat, lru, hf_last, h0_b, s_w, s_bias,
                       ts=2 * tm_lat, tc=tm_lat, ts_bwd=4 * tm_lat)
        h = _ffn(h, m, 6, ffn2_norm_g[l], w2_in, w2_out, l, tm=2 * tm_lat,
                 outproj=(y, w_mix_out), final_g=final_norm_g if last else None)

    return h.reshape(bsz, n_lat, d)
```
